```python
import jax, jax.numpy as jnp
from jax import lax
import numpy as np

D_MODEL = 1024
BATCH = 4
SEQ = 4096
DEPTH = 4
DEC_BATCH = 128
DEC_SEQ = 4
PAST_LEN = 2048
PAGE_SIZE = 128

N_MIXERS = 2
N_GLA_LAYERS = (DEPTH + 1) // 2
N_FOX_LAYERS = DEPTH // 2

GLA_HEADS = 4
GLA_DK = D_MODEL // 2 // GLA_HEADS
GLA_DV = D_MODEL // GLA_HEADS
GLA_KDIM = GLA_HEADS * GLA_DK
GLA_VDIM = GLA_HEADS * GLA_DV
GLA_RANK = 16
GLA_TAU = 16.0
GLA_CHUNK = 64
GLA_SPLITS = [GLA_KDIM, 2 * GLA_KDIM, 2 * GLA_KDIM + GLA_VDIM, 2 * GLA_KDIM + 2 * GLA_VDIM]
GLA_IN = 2 * GLA_KDIM + 2 * GLA_VDIM + GLA_RANK

FOX_HEADS = 8
FOX_HD = D_MODEL // FOX_HEADS
FOX_WIDTH = FOX_HEADS * FOX_HD
FOX_SPLITS = [FOX_WIDTH, 2 * FOX_WIDTH, 3 * FOX_WIDTH, 4 * FOX_WIDTH]
FOX_IN = 4 * FOX_WIDTH + FOX_HEADS
Q_BLOCK = 128

EPS = 1e-6

kernel_name = "hybrid_gla_fox_decoder_step"


def rmsnorm(x, g):
    xf = x.astype(jnp.float32)
    y = xf * lax.rsqrt(jnp.mean(xf * xf, axis=-1, keepdims=True) + EPS) * g.astype(jnp.float32)
    return y.astype(x.dtype)


def gla_chunk_step(S, inp):
    q, k, v, g = inp
    C = q.shape[1]
    b = jnp.cumsum(g, axis=1)
    o_inter = jnp.einsum('bchk,bhkv->bchv', q * jnp.exp(b), S)
    diff = b[:, :, None] - b[:, None, :]
    causal = jnp.tril(jnp.ones((C, C), dtype=bool))[None, :, :, None, None]
    decay = jnp.exp(jnp.where(causal, diff, -jnp.inf))
    A = jnp.sum(q[:, :, None] * k[:, None, :] * decay, axis=-1)
    o_intra = jnp.einsum('btsh,bshv->bthv', A, v)
    b_last = b[:, -1]
    k_dec = k * jnp.exp(b_last[:, None] - b)
    S_new = jnp.exp(b_last)[..., None] * S + jnp.einsum('bshk,bshv->bhkv', k_dec, v)
    return S_new, o_inter + o_intra


def gla_recurrence(q, k, v, g, S0, chunk):
    B, L = q.shape[:2]
    n = L // chunk
    def to_chunks(a):
        return jnp.moveaxis(a.reshape((B, n, chunk) + a.shape[2:]), 1, 0)
    S, o = lax.scan(gla_chunk_step, S0, (to_chunks(q), to_chunks(k), to_chunks(v), to_chunks(g)))
    o = jnp.moveaxis(o, 0, 1).reshape(B, L, GLA_HEADS, GLA_DV)
    return S, o


def gla_mixer(h, S0, w_in, w_gate2, b_gate, g_out, w_out, chunk):
    B, L, _ = h.shape
    f32 = jnp.float32
    q, k, v, z, lr = jnp.split(h @ w_in, GLA_SPLITS, axis=-1)
    q = q.reshape(B, L, GLA_HEADS, GLA_DK).astype(f32) * (GLA_DK ** -0.5)
    k = k.reshape(B, L, GLA_HEADS, GLA_DK).astype(f32)
    v = v.reshape(B, L, GLA_HEADS, GLA_DV).astype(f32)
    g = jax.nn.log_sigmoid((lr @ w_gate2 + b_gate).astype(f32)) / GLA_TAU
    g = g.reshape(B, L, GLA_HEADS, GLA_DK)
    S, o = gla_recurrence(q, k, v, g, S0.astype(f32), chunk)
    o = rmsnorm(o, g_out).reshape(B, L, GLA_VDIM).astype(h.dtype) * jax.nn.silu(z)
    return o @ w_out, S.astype(S0.dtype)


def fox_project(h, w_in, b_f, g_q, g_k):
    B, L, _ = h.shape
    q, k, v, z, f = jnp.split(h @ w_in, FOX_SPLITS, axis=-1)
    q = rmsnorm(q.reshape(B, L, FOX_HEADS, FOX_HD), g_q)
    k = rmsnorm(k.reshape(B, L, FOX_HEADS, FOX_HD), g_k)
    v = v.reshape(B, L, FOX_HEADS, FOX_HD)
    logf = jax.nn.log_sigmoid((f + b_f).astype(jnp.float32))
    return q, k, v, z, logf


def fox_attend_prompt(q, k, v, logf):
    B, L, H, HD = q.shape
    scale = HD ** -0.5
    cT = jnp.cumsum(logf, axis=1).transpose(0, 2, 1)
    nb = L // Q_BLOCK
    qb = q.reshape(B, nb, Q_BLOCK, H, HD).swapaxes(0, 1)
    cb = cT.reshape(B, H, nb, Q_BLOCK).transpose(2, 0, 1, 3)
    starts = jnp.arange(nb, dtype=jnp.int32) * Q_BLOCK
    kpos = jnp.arange(L, dtype=jnp.int32)

    def block(args):
        q_i, c_i, s0 = args
        logits = jnp.einsum('bqhd,bkhd->bhqk', q_i, k).astype(jnp.float32) * scale
        logits = logits + (c_i[..., None] - cT[:, :, None, :])
        qpos = s0 + jnp.arange(Q_BLOCK, dtype=jnp.int32)
        mask = kpos[None, :] <= qpos[:, None]
        p = jax.nn.softmax(jnp.where(mask, logits, -jnp.inf), axis=-1).astype(v.dtype)
        return jnp.einsum('bhqk,bkhd->bqhd', p, v)

    o = lax.map(block, (qb, cb, starts))
    return o.swapaxes(0, 1).reshape(B, L, H, HD)


def fox_attend_sample(q, k, v, logf, k_past, v_past, logf_past):
    T = q.shape[1]
    P = k_past.shape[1]
    scale = q.shape[-1] ** -0.5
    c_past = jnp.cumsum(logf_past.astype(jnp.float32), axis=1)
    c_new = c_past[:, -1:] + jnp.cumsum(logf, axis=1)
    cpT = c_past.transpose(0, 2, 1)
    cnT = c_new.transpose(0, 2, 1)
    l_past = jnp.einsum('bqhd,bkhd->bhqk', q, k_past).astype(jnp.float32) * scale
    l_past = l_past + (cnT[..., None] - cpT[:, :, None, :])
    l_new = jnp.einsum('bqhd,bkhd->bhqk', q, k).astype(jnp.float32) * scale
    l_new = l_new + (cnT[..., None] - cnT[:, :, None, :])
    causal = jnp.tril(jnp.ones((T, T), dtype=bool))
    l_new = jnp.where(causal, l_new, -jnp.inf)
    p = jax.nn.softmax(jnp.concatenate([l_past, l_new], axis=-1), axis=-1).astype(v.dtype)
    return (jnp.einsum('bhqk,bkhd->bqhd', p[..., :P], v_past)
            + jnp.einsum('bhqk,bkhd->bqhd', p[..., P:], v))


def fox_out(o, z, g_o, w_out):
    B, L = o.shape[:2]
    o = rmsnorm(o, g_o).reshape(B, L, FOX_WIDTH) * jax.nn.silu(z)
    return o @ w_out


def setup_inputs(seed: int = 0) -> dict:
    key = jax.random.key(seed)
    ks = jax.random.split(key, 24)
    f32 = jnp.float32
    n_pages = PAST_LEN // PAGE_SIZE
    n_used = DEC_BATCH * n_pages
    n_phys = n_used + (n_used + 3) // 4

    def nrm(k, shape, scale=1.0):
        return jax.random.normal(k, shape, f32) * scale

    fox_bias_base = jnp.linspace(1.0, 6.0, FOX_HEADS, dtype=f32)
    x_prompt = nrm(ks[0], (BATCH, SEQ, D_MODEL))
    x_sample = nrm(ks[1], (DEC_BATCH, DEC_SEQ, D_MODEL))
    state_gla = nrm(ks[2], (N_GLA_LAYERS, DEC_BATCH, GLA_HEADS, GLA_DK, GLA_DV))
    cache_k = nrm(ks[3], (N_FOX_LAYERS, n_phys, PAGE_SIZE, FOX_HEADS, FOX_HD))
    cache_v = nrm(ks[4], (N_FOX_LAYERS, n_phys, PAGE_SIZE, FOX_HEADS, FOX_HD))
    cache_logf = jax.nn.log_sigmoid(fox_bias_base + nrm(ks[5], (N_FOX_LAYERS, n_phys, PAGE_SIZE, FOX_HEADS), 0.5))
    page_table = jax.random.permutation(ks[6], n_phys)[:n_used].reshape(DEC_BATCH, n_pages).astype(jnp.int32)

    norm_pre = 1.0 + nrm(ks[7], (DEPTH, D_MODEL), 0.02)
    norm_post = 1.0 + nrm(ks[8], (DEPTH, D_MODEL), 0.02)

    gla_w_in = nrm(ks[9], (N_GLA_LAYERS, D_MODEL, GLA_IN), D_MODEL ** -0.5)
    gla_w_gate2 = nrm(ks[10], (N_GLA_LAYERS, GLA_RANK, GLA_KDIM), GLA_RANK ** -0.5)
    gla_b_gate = 0.5 + nrm(ks[11], (N_GLA_LAYERS, GLA_KDIM), 0.1)
    gla_norm_o = 1.0 + nrm(ks[12], (N_GLA_LAYERS, GLA_DV), 0.02)
    gla_w_out = nrm(ks[13], (N_GLA_LAYERS, GLA_VDIM, D_MODEL), GLA_VDIM ** -0.5)

    fox_w_in = nrm(ks[14], (N_FOX_LAYERS, D_MODEL, FOX_IN), D_MODEL ** -0.5)
    fox_b_f = fox_bias_base + nrm(ks[15], (N_FOX_LAYERS, FOX_HEADS), 0.01)
    fox_norm_q = 1.0 + nrm(ks[16], (N_FOX_LAYERS, FOX_HD), 0.02)
    fox_norm_k = 1.0 + nrm(ks[17], (N_FOX_LAYERS, FOX_HD), 0.02)
    fox_norm_o = 1.0 + nrm(ks[18], (N_FOX_LAYERS, FOX_HD), 0.02)
    fox_w_out = nrm(ks[19], (N_FOX_LAYERS, FOX_WIDTH, D_MODEL), FOX_WIDTH ** -0.5)

    return {"x_prompt": x_prompt, "x_sample": x_sample, "state_gla": state_gla,
            "cache_k": cache_k, "cache_v": cache_v, "cache_logf": cache_logf,
            "page_table": page_table, "norm_pre": norm_pre, "norm_post": norm_post,
            "gla_w_in": gla_w_in, "gla_w_gate2": gla_w_gate2, "gla_b_gate": gla_b_gate,
            "gla_norm_o": gla_norm_o, "gla_w_out": gla_w_out,
            "fox_w_in": fox_w_in, "fox_b_f": fox_b_f, "fox_norm_q": fox_norm_q,
            "fox_norm_k": fox_norm_k, "fox_norm_o": fox_norm_o, "fox_w_out": fox_w_out}


def reference(x_prompt, x_sample, state_gla, cache_k, cache_v, cache_logf, page_table,
              norm_pre, norm_post, gla_w_in, gla_w_gate2, gla_b_gate, gla_norm_o, gla_w_out,
              fox_w_in, fox_b_f, fox_norm_q, fox_norm_k, fox_norm_o, fox_w_out):
    xp, xs = x_prompt, x_sample
    bp, lp = xp.shape[0], xp.shape[1]
    bs, ls = xs.shape[0], xs.shape[1]
    gla_p, gla_s = [], []
    kp, vp, fp, ksl, vsl, fsl = [], [], [], [], [], []
    for i in range(DEPTH):
        j = i // N_MIXERS
        hp = rmsnorm(xp, norm_pre[i])
        hs = rmsnorm(xs, norm_pre[i])
        if i % N_MIXERS == 0:
            S0p = jnp.zeros((bp, GLA_HEADS, GLA_DK, GLA_DV), xp.dtype)
            op, Sp = gla_mixer(hp, S0p, gla_w_in[j], gla_w_gate2[j], gla_b_gate[j],
                               gla_norm_o[j], gla_w_out[j], min(GLA_CHUNK, lp))
            os_, Ss = gla_mixer(hs, state_gla[j], gla_w_in[j], gla_w_gate2[j], gla_b_gate[j],
                                gla_norm_o[j], gla_w_out[j], ls)
            gla_p.append(Sp)
            gla_s.append(Ss)
        else:
            q, k, v, z, lf = fox_project(hp, fox_w_in[j], fox_b_f[j], fox_norm_q[j], fox_norm_k[j])
            op = fox_out(fox_attend_prompt(q, k, v, lf), z, fox_norm_o[j], fox_w_out[j])
            kp.append(k)
            vp.append(v)
            fp.append(lf.astype(xp.dtype))
            q, k, v, z, lf = fox_project(hs, fox_w_in[j], fox_b_f[j], fox_norm_q[j], fox_norm_k[j])
            k_past = cache_k[j][page_table].reshape(bs, -1, FOX_HEADS, FOX_HD)
            v_past = cache_v[j][page_table].reshape(bs, -1, FOX_HEADS, FOX_HD)
            f_past = cache_logf[j][page_table].reshape(bs, -1, FOX_HEADS)
            os_ = fox_out(fox_attend_sample(q, k, v, lf, k_past, v_past, f_past), z,
                          fox_norm_o[j], fox_w_out[j])
            ksl.append(k)
            vsl.append(v)
            fsl.append(lf.astype(cache_logf.dtype))
        xp = xp + rmsnorm(op, norm_post[i])
        xs = xs + rmsnorm(os_, norm_post[i])
    return (xp, xs, jnp.stack(gla_p), jnp.stack(gla_s), jnp.stack(kp), jnp.stack(vp), jnp.stack(fp),
            jnp.stack(ksl), jnp.stack(vsl), jnp.stack(fsl))
```

```python
import functools

import numpy as np
import jax
import jax.numpy as jnp
from jax import lax
from jax.experimental import pallas as pl
from jax.experimental.pallas import tpu as pltpu

F32 = jnp.float32
BF16 = jnp.bfloat16
EPS = 1e-6

D_MODEL = 1024
GLA_HEADS = 4
GLA_DK = 128
GLA_DV = 256
GLA_KDIM = GLA_HEADS * GLA_DK
GLA_VDIM = GLA_HEADS * GLA_DV
GLA_RANK = 16
GLA_TAU = 16.0
FOX_HEADS = 8
FOX_HD = 128
FOX_WIDTH = FOX_HEADS * FOX_HD
PAGE_SIZE = 128

V7X_VMEM_LIMIT_BYTES = 56 * 1024 * 1024
LANES = 128

ROW_TILE = 512
GLA_CHUNK = 128
GLA_HALF = GLA_CHUNK // 2
GLA_BLOCK = 512
ATT_BLOCK = 512


def _params(n_axes):
    return pltpu.CompilerParams(
        dimension_semantics=("arbitrary",) * n_axes,
        vmem_limit_bytes=V7X_VMEM_LIMIT_BYTES,
    )


def _dot(a, b):
    return jnp.dot(a, b, preferred_element_type=F32)


def _dot_nt(a, b):
    return lax.dot_general(a, b, (((1,), (1,)), ((), ())), preferred_element_type=F32)


def _dot_tn(a, b):
    return lax.dot_general(a, b, (((0,), (0,)), ((), ())), preferred_element_type=F32)


def _dot_exact(a, b):
    return jnp.dot(a, b, precision=lax.Precision.HIGHEST, preferred_element_type=F32)


def _log_sigmoid(x):
    return jnp.minimum(x, 0.0) - jnp.log1p(jnp.exp(-jnp.abs(x)))


def _silu(x):
    return x * (1.0 / (1.0 + jnp.exp(-x)))


def _rms_rows(x):
    return x * lax.rsqrt(jnp.mean(x * x, axis=-1, keepdims=True) + EPS)


def _group_rmsnorm(y, gain_row, group):
    outs = []
    for g0 in range(0, y.shape[-1], group):
        yg = y[:, g0:g0 + group]
        outs.append(_rms_rows(yg) * gain_row[:, g0:g0 + group])
    return jnp.concatenate(outs, axis=-1)


def _tri(n, lower):
    r = lax.broadcasted_iota(jnp.int32, (n, n), 0)
    c = lax.broadcasted_iota(jnp.int32, (n, n), 1)
    return (r >= c) if lower else (r <= c)


def _full(shape):
    nd = len(shape)
    return pl.BlockSpec(shape, lambda *_: (0,) * nd)


def _gla_proj_kernel(x_ref, gpre_ref, wq_ref, wk_ref, wv_ref, wz_ref, wlr_ref, wg2_ref, bg_ref,
                     q_ref, k_ref, v_ref, z_ref, g_ref):
    hb = (_rms_rows(x_ref[...]) * gpre_ref[...]).astype(BF16)
    q_ref[...] = _dot(hb, wq_ref[...]) * (GLA_DK ** -0.5)
    k_ref[...] = _dot(hb, wk_ref[...])
    v_ref[...] = _dot(hb, wv_ref[...])
    z_ref[...] = _dot(hb, wz_ref[...])
    lr = _dot(hb, wlr_ref[...])
    xg = _dot(lr.astype(BF16), wg2_ref[...]) + bg_ref[...]
    g_ref[...] = _log_sigmoid(xg) * (1.0 / GLA_TAU)


def _gla_proj(x, gpre, wq, wk, wv, wz, wlr, wg2, bg):
    n = x.shape[0]
    tm = min(ROW_TILE, n)
    row = lambda w: pl.BlockSpec((tm, w), lambda i: (i, 0))
    return pl.pallas_call(
        _gla_proj_kernel,
        grid=(n // tm,),
        in_specs=[row(D_MODEL), _full(gpre.shape), _full(wq.shape), _full(wk.shape), _full(wv.shape),
                  _full(wz.shape), _full(wlr.shape), _full(wg2.shape), _full(bg.shape)],
        out_specs=[row(GLA_KDIM), row(GLA_KDIM), row(GLA_VDIM), row(GLA_VDIM), row(GLA_KDIM)],
        out_shape=[jax.ShapeDtypeStruct((n, GLA_KDIM), F32), jax.ShapeDtypeStruct((n, GLA_KDIM), F32),
                   jax.ShapeDtypeStruct((n, GLA_VDIM), F32), jax.ShapeDtypeStruct((n, GLA_VDIM), F32),
                   jax.ShapeDtypeStruct((n, GLA_KDIM), F32)],
        compiler_params=_params(1),
        name="gla_proj",
    )(x, gpre, wq, wk, wv, wz, wlr, wg2, bg)


def _gla_intra_scores(q, k, b):
    h = GLA_HALF
    q0, q1 = q[:h], q[h:]
    k0, k1 = k[:h], k[h:]
    b0, b1 = b[:h], b[h:]
    r0 = b[h // 2 - 1:h // 2]
    rb = b[h - 1:h]
    r1 = b[h + h // 2 - 1:h + h // 2]
    tril = _tri(h, lower=True)
    a00 = _dot_nt((q0 * jnp.exp(b0 - r0)).astype(BF16), (k0 * jnp.exp(r0 - b0)).astype(BF16))
    a11 = _dot_nt((q1 * jnp.exp(b1 - r1)).astype(BF16), (k1 * jnp.exp(r1 - b1)).astype(BF16))
    a10 = _dot_nt((q1 * jnp.exp(b1 - rb)).astype(BF16), (k0 * jnp.exp(rb - b0)).astype(BF16))
    a00 = jnp.where(tril, a00, 0.0)
    a11 = jnp.where(tril, a11, 0.0)
    top = jnp.concatenate([a00, jnp.zeros_like(a00)], axis=1)
    bot = jnp.concatenate([a10, a11], axis=1)
    return jnp.concatenate([top, bot], axis=0)


def _gla_prompt_kernel(q_ref, k_ref, v_ref, g_ref, o_ref, s_ref):
    @pl.when(pl.program_id(1) == 0)
    def _():
        s_ref[...] = jnp.zeros_like(s_ref)

    c = GLA_CHUNK
    tril = _tri(c, lower=True).astype(F32)
    for ci in range(GLA_BLOCK // c):
        rows = slice(ci * c, (ci + 1) * c)
        for h in range(GLA_HEADS):
            kcol = slice(h * GLA_DK, (h + 1) * GLA_DK)
            vcol = slice(h * GLA_DV, (h + 1) * GLA_DV)
            q = q_ref[0, rows, kcol]
            k = k_ref[0, rows, kcol]
            v = v_ref[0, rows, vcol].astype(BF16)
            b = _dot_exact(tril, g_ref[0, rows, kcol])
            b_last = b[c - 1:c]
            s_old = s_ref[0, h]
            o_inter = _dot((q * jnp.exp(b)).astype(BF16), s_old.astype(BF16))
            a = _gla_intra_scores(q, k, b)
            o_ref[0, rows, vcol] = o_inter + _dot(a.astype(BF16), v)
            k_dec = (k * jnp.exp(b_last - b)).astype(BF16)
            e_col = jnp.broadcast_to(jnp.exp(b_last), (GLA_DK, GLA_DK)).T
            e_col = jnp.concatenate([e_col, e_col], axis=1)
            s_ref[0, h] = e_col * s_old + _dot_tn(k_dec, v)


def _gla_prompt(q, k, v, g):
    b, l, _ = q.shape
    blk = lambda w: pl.BlockSpec((1, GLA_BLOCK, w), lambda bi, i: (bi, i, 0))
    return pl.pallas_call(
        _gla_prompt_kernel,
        grid=(b, l // GLA_BLOCK),
        in_specs=[blk(GLA_KDIM), blk(GLA_KDIM), blk(GLA_VDIM), blk(GLA_KDIM)],
        out_specs=[blk(GLA_VDIM),
                   pl.BlockSpec((1, GLA_HEADS, GLA_DK, GLA_DV), lambda bi, i: (bi, 0, 0, 0))],
        out_shape=[jax.ShapeDtypeStruct((b, l, GLA_VDIM), F32),
                   jax.ShapeDtypeStruct((b, GLA_HEADS, GLA_DK, GLA_DV), F32)],
        compiler_params=_params(2),
        name="gla_prompt",
    )(q, k, v, g)


def _gla_sample_kernel(q_ref, k_ref, v_ref, g_ref, s0_ref, o_ref, s_ref):
    t = q_ref.shape[1]
    tril = _tri(t, lower=True)
    g = g_ref[0]
    b = _dot_exact(tril.astype(F32), g)
    b_last = b[t - 1:t]
    q = q_ref[0]
    k = k_ref[0]
    qe = q * jnp.exp(b)
    ke = k * jnp.exp(-b)
    k_dec = k * jnp.exp(b_last - b)
    e_last = jnp.exp(b_last)
    for h in range(GLA_HEADS):
        kcol = slice(h * GLA_DK, (h + 1) * GLA_DK)
        vcol = slice(h * GLA_DV, (h + 1) * GLA_DV)
        v = v_ref[0, :, vcol].astype(BF16)
        s_old = s0_ref[0, 0, h]
        a = _dot_nt(qe[:, kcol].astype(BF16), ke[:, kcol].astype(BF16))
        a = jnp.where(tril, a, 0.0)
        o_ref[0, :, vcol] = _dot(qe[:, kcol].astype(BF16), s_old.astype(BF16)) + _dot(a.astype(BF16), v)
        e_col = jnp.broadcast_to(e_last[:, kcol], (GLA_DK, GLA_DK)).T
        e_col = jnp.concatenate([e_col, e_col], axis=1)
        s_ref[0, h] = e_col * s_old + _dot_tn(k_dec[:, kcol].astype(BF16), v)


def _gla_sample(q, k, v, g, state, j):
    n, t, _ = q.shape
    blk = lambda w: pl.BlockSpec((1, t, w), lambda i: (i, 0, 0))
    return pl.pallas_call(
        _gla_sample_kernel,
        grid=(n,),
        in_specs=[blk(GLA_KDIM), blk(GLA_KDIM), blk(GLA_VDIM), blk(GLA_KDIM),
                  pl.BlockSpec((1, 1, GLA_HEADS, GLA_DK, GLA_DV), lambda i: (j, i, 0, 0, 0))],
        out_specs=[blk(GLA_VDIM),
                   pl.BlockSpec((1, GLA_HEADS, GLA_DK, GLA_DV), lambda i: (i, 0, 0, 0))],
        out_shape=[jax.ShapeDtypeStruct((n, t, GLA_VDIM), F32),
                   jax.ShapeDtypeStruct((n, GLA_HEADS, GLA_DK, GLA_DV), F32)],
        compiler_params=_params(1),
        name="gla_sample",
    )(q, k, v, g, state)


def _fox_proj_kernel(x_ref, gpre_ref, wq_ref, wk_ref, wv_ref, wz_ref, wf_ref, wft_ref, bf_ref, bft_ref,
                     gq_ref, gk_ref, q_ref, k_ref, v_ref, z_ref, lf_ref, lft_ref):
    hb = (_rms_rows(x_ref[...]) * gpre_ref[...]).astype(BF16)
    q_ref[...] = _group_rmsnorm(_dot(hb, wq_ref[...]), gq_ref[...], FOX_HD)
    k_ref[...] = _group_rmsnorm(_dot(hb, wk_ref[...]), gk_ref[...], FOX_HD)
    v_ref[...] = _dot(hb, wv_ref[...])
    z_ref[...] = _dot(hb, wz_ref[...])
    lf_ref[...] = _log_sigmoid(_dot(hb, wf_ref[...]) + bf_ref[...])
    lft_ref[...] = _log_sigmoid(_dot_nt(wft_ref[...], hb) + bft_ref[...])


def _fox_proj(x, gpre, wq, wk, wv, wz, wf, wft, bf, bft, gq, gk):
    n = x.shape[0]
    tm = min(ROW_TILE, n)
    row = lambda w: pl.BlockSpec((tm, w), lambda i: (i, 0))
    wide = jax.ShapeDtypeStruct((n, FOX_WIDTH), F32)
    return pl.pallas_call(
        _fox_proj_kernel,
        grid=(n // tm,),
        in_specs=[row(D_MODEL), _full(gpre.shape), _full(wq.shape), _full(wk.shape), _full(wv.shape),
                  _full(wz.shape), _full(wf.shape), _full(wft.shape), _full(bf.shape), _full(bft.shape),
                  _full(gq.shape), _full(gk.shape)],
        out_specs=[row(FOX_WIDTH), row(FOX_WIDTH), row(FOX_WIDTH), row(FOX_WIDTH), row(FOX_HEADS),
                   pl.BlockSpec((FOX_HEADS, tm), lambda i: (0, i))],
        out_shape=[wide, wide, wide, wide, jax.ShapeDtypeStruct((n, FOX_HEADS), F32),
                   jax.ShapeDtypeStruct((FOX_HEADS, n), F32)],
        compiler_params=_params(1),
        name="fox_proj",
    )(x, gpre, wq, wk, wv, wz, wf, wft, bf, bft, gq, gk)


def _fox_cumsum_kernel(lf_ref, lft_ref, c_ref, crow_ref):
    l = lf_ref.shape[1]
    tril = _tri(LANES, lower=True).astype(F32)
    triu = _tri(LANES, lower=False).astype(F32)
    carry_row = jnp.zeros((1, FOX_HEADS), F32)
    carry_col = jnp.zeros((FOX_HEADS, 1), F32)
    for i in range(l // LANES):
        sl = slice(i * LANES, (i + 1) * LANES)
        c = _dot_exact(tril, lf_ref[0, sl, :]) + carry_row
        carry_row = c[LANES - 1:LANES, :]
        c_ref[0, sl, :] = c
        ct = _dot_exact(lft_ref[:, sl], triu) + carry_col
        carry_col = ct[:, LANES - 1:LANES]
        for h in range(FOX_HEADS):
            crow_ref[0, h, :, sl] = ct[h:h + 1, :]


def _fox_cumsum(lf, lft):
    b, l, _ = lf.shape
    return pl.pallas_call(
        _fox_cumsum_kernel,
        grid=(b,),
        in_specs=[pl.BlockSpec((1, l, FOX_HEADS), lambda i: (i, 0, 0)),
                  pl.BlockSpec((FOX_HEADS, l), lambda i: (0, i))],
        out_specs=[pl.BlockSpec((1, l, FOX_HEADS), lambda i: (i, 0, 0)),
                   pl.BlockSpec((1, FOX_HEADS, 1, l), lambda i: (i, 0, 0, 0))],
        out_shape=[jax.ShapeDtypeStruct((b, l, FOX_HEADS), F32),
                   jax.ShapeDtypeStruct((b, FOX_HEADS, 1, l), F32)],
        compiler_params=_params(1),
        name="fox_cumsum",
    )(lf, lft)


def _fox_flash_kernel(qi_ref, ki_ref, q_ref, k_ref, v_ref, c_ref, crow_ref, o_ref,
                      m_ref, l_ref, acc_ref, ccol_ref):
    t = pl.program_id(2)
    h = pl.program_id(1)
    qi = qi_ref[t]
    ki = ki_ref[t]

    @pl.when(ki == 0)
    def _():
        m_ref[...] = jnp.full_like(m_ref, -jnp.inf)
        l_ref[...] = jnp.zeros_like(l_ref)
        acc_ref[...] = jnp.zeros_like(acc_ref)
        lane = lax.broadcasted_iota(jnp.int32, c_ref.shape[1:], 1)
        ccol_ref[...] = jnp.sum(jnp.where(lane == h, c_ref[0], 0.0), axis=-1, keepdims=True)

    qb = (q_ref[0] * (FOX_HD ** -0.5)).astype(BF16)
    s = _dot_nt(qb, k_ref[0].astype(BF16)) + (ccol_ref[...] - crow_ref[0, 0])

    def update(s):
        m_old = m_ref[...]
        m_new = jnp.maximum(m_old, jnp.max(s, axis=-1, keepdims=True))
        alpha = jnp.exp(m_old - m_new)
        p = jnp.exp(s - m_new)
        l_ref[...] = alpha * l_ref[...] + jnp.sum(p, axis=-1, keepdims=True)
        acc_ref[...] = alpha * acc_ref[...] + _dot(p.astype(BF16), v_ref[0].astype(BF16))
        m_ref[...] = m_new

    @pl.when(ki < qi)
    def _():
        update(s)

    @pl.when(ki == qi)
    def _():
        update(jnp.where(_tri(ATT_BLOCK, lower=True), s, -jnp.inf))
        o_ref[0] = acc_ref[...] / l_ref[...]


def _fox_flash(q, k, v, c, crow):
    b, l, _ = q.shape
    nq = l // ATT_BLOCK
    qi = np.array([i for i in range(nq) for _ in range(i + 1)], np.int32)
    ki = np.array([j for i in range(nq) for j in range(i + 1)], np.int32)
    qspec = pl.BlockSpec((1, ATT_BLOCK, FOX_HD), lambda bi, h, t, qi, ki: (bi, qi[t], h))
    kspec = pl.BlockSpec((1, ATT_BLOCK, FOX_HD), lambda bi, h, t, qi, ki: (bi, ki[t], h))
    grid_spec = pltpu.PrefetchScalarGridSpec(
        num_scalar_prefetch=2,
        grid=(b, FOX_HEADS, len(qi)),
        in_specs=[qspec, kspec, kspec,
                  pl.BlockSpec((1, ATT_BLOCK, FOX_HEADS), lambda bi, h, t, qi, ki: (bi, qi[t], 0)),
                  pl.BlockSpec((1, 1, 1, ATT_BLOCK), lambda bi, h, t, qi, ki: (bi, h, 0, ki[t]))],
        out_specs=qspec,
        scratch_shapes=[pltpu.VMEM((ATT_BLOCK, 1), F32), pltpu.VMEM((ATT_BLOCK, 1), F32),
                        pltpu.VMEM((ATT_BLOCK, FOX_HD), F32), pltpu.VMEM((ATT_BLOCK, 1), F32)],
    )
    return pl.pallas_call(
        _fox_flash_kernel,
        grid_spec=grid_spec,
        out_shape=jax.ShapeDtypeStruct((b, l, FOX_WIDTH), F32),
        compiler_params=_params(3),
        name="fox_flash",
    )(jnp.asarray(qi), jnp.asarray(ki), q, k, v, c, crow)


def _fox_decode_kernel(n_pages, pt_ref, q_ref, kn_ref, vn_ref, lfn_ref, *refs):
    del pt_ref
    kp_refs = refs[:n_pages]
    vp_refs = refs[n_pages:2 * n_pages]
    lf_refs = refs[2 * n_pages:3 * n_pages]
    o_ref = refs[3 * n_pages]
    t = q_ref.shape[1]
    rows = t * FOX_HEADS

    head_of_lane = lax.broadcasted_iota(jnp.int32, (FOX_HEADS, FOX_WIDTH), 1) // FOX_HD
    head_mask = head_of_lane == lax.broadcasted_iota(jnp.int32, (FOX_HEADS, FOX_WIDTH), 0)
    q = q_ref[0] * (FOX_HD ** -0.5)
    qbd = jnp.concatenate(
        [jnp.where(head_mask, jnp.broadcast_to(q[i:i + 1], (FOX_HEADS, FOX_WIDTH)), 0.0) for i in range(t)],
        axis=0).astype(BF16)

    triu = _tri(LANES, lower=False).astype(F32)
    carry = jnp.zeros((FOX_HEADS, 1), F32)
    cps = []
    for p in range(n_pages):
        cp = _dot_exact(lf_refs[p][0, 0], triu) + carry
        carry = cp[:, LANES - 1:LANES]
        cps.append(cp)
    c_total = carry
    cnew = _dot_exact(lfn_ref[0], triu)
    cq = jnp.concatenate([cnew[:, i:i + 1] for i in range(t)], axis=0)

    logits = []
    for p in range(n_pages):
        s = _dot_nt(qbd, kp_refs[p][0, 0].astype(BF16))
        bias = jnp.concatenate([c_total - cps[p]] * t, axis=0)
        logits.append(s + bias + cq)
    s_new = _dot_nt(qbd, kn_ref[0].astype(BF16))
    bias_new = cq - jnp.concatenate([cnew[:, :t]] * t, axis=0)
    key_id = lax.broadcasted_iota(jnp.int32, (rows, t), 1)
    tok_id = lax.broadcasted_iota(jnp.int32, (rows, t), 0) // FOX_HEADS
    l_new = jnp.where(key_id <= tok_id, s_new + bias_new, -jnp.inf)

    m = jnp.max(l_new, axis=-1, keepdims=True)
    for lg in logits:
        m = jnp.maximum(m, jnp.max(lg, axis=-1, keepdims=True))
    p_new = jnp.exp(l_new - m)
    denom = jnp.sum(p_new, axis=-1, keepdims=True)
    vn = vn_ref[0]
    acc = jnp.zeros((rows, FOX_WIDTH), F32)
    for i in range(t):
        acc = acc + p_new[:, i:i + 1] * vn[i:i + 1, :]
    for p in range(n_pages):
        pp = jnp.exp(logits[p] - m)
        denom = denom + jnp.sum(pp, axis=-1, keepdims=True)
        acc = acc + _dot(pp.astype(BF16), vp_refs[p][0, 0].astype(BF16))
    acc = acc / denom
    o_ref[0] = jnp.concatenate(
        [jnp.sum(jnp.where(head_mask, acc[i * FOX_HEADS:(i + 1) * FOX_HEADS], 0.0), axis=0, keepdims=True)
         for i in range(t)], axis=0)


def _fox_decode(q, kn, vn, lfn_t, cache_k, cache_v, cache_lft, page_table, j):
    n, t, _ = q.shape
    n_pages = page_table.shape[1]
    tok = pl.BlockSpec((1, t, FOX_WIDTH), lambda i, pt: (i, 0, 0))

    def page_spec(p, shape):
        return pl.BlockSpec((1, 1) + shape, lambda i, pt: (j, pt[i * n_pages + p], 0, 0))

    in_specs = ([tok, tok, tok, pl.BlockSpec((1, FOX_HEADS, LANES), lambda i, pt: (i, 0, 0))]
                + [page_spec(p, (PAGE_SIZE, FOX_WIDTH)) for p in range(n_pages)]
                + [page_spec(p, (PAGE_SIZE, FOX_WIDTH)) for p in range(n_pages)]
                + [page_spec(p, (FOX_HEADS, PAGE_SIZE)) for p in range(n_pages)])
    grid_spec = pltpu.PrefetchScalarGridSpec(
        num_scalar_prefetch=1, grid=(n,), in_specs=in_specs, out_specs=tok)
    return pl.pallas_call(
        functools.partial(_fox_decode_kernel, n_pages),
        grid_spec=grid_spec,
        out_shape=jax.ShapeDtypeStruct((n, t, FOX_WIDTH), F32),
        compiler_params=_params(1),
        name="fox_decode",
    )(page_table.reshape(-1), q, kn, vn, lfn_t,
      *([cache_k] * n_pages), *([cache_v] * n_pages), *([cache_lft] * n_pages))


def _out_kernel(group, o_ref, z_ref, x_ref, gh_ref, w_ref, gpost_ref, y_ref):
    gated = _group_rmsnorm(o_ref[...], gh_ref[...], group) * _silu(z_ref[...])
    y = _dot(gated.astype(BF16), w_ref[...])
    y_ref[...] = x_ref[...] + _rms_rows(y) * gpost_ref[...]


def _out_stage(o, z, x, gh, w, gpost, group):
    n = x.shape[0]
    tm = min(ROW_TILE, n)
    row = pl.BlockSpec((tm, D_MODEL), lambda i: (i, 0))
    return pl.pallas_call(
        functools.partial(_out_kernel, group),
        grid=(n // tm,),
        in_specs=[row, row, row, _full(gh.shape), _full(w.shape), _full(gpost.shape)],
        out_specs=row,
        out_shape=jax.ShapeDtypeStruct((n, D_MODEL), F32),
        compiler_params=_params(1),
        name="out_stage",
    )(o, z, x, gh, w, gpost)


def _gla_layer(xp, xs, state_gla, j, gpre, gpost, w_in, w_gate2, b_gate, g_out, w_out):
    bp, lp, _ = xp.shape
    bs, ls, _ = xs.shape
    kd, vd = GLA_KDIM, GLA_VDIM
    wb = w_in.astype(BF16)
    weights = (gpre.reshape(1, -1), wb[:, :kd], wb[:, kd:2 * kd], wb[:, 2 * kd:2 * kd + vd],
               wb[:, 2 * kd + vd:2 * kd + 2 * vd], wb[:, 2 * kd + 2 * vd:], w_gate2.astype(BF16),
               b_gate.reshape(1, -1))
    gh = jnp.tile(g_out, GLA_HEADS).reshape(1, -1)
    w_out_b = w_out.astype(BF16)
    gpost = gpost.reshape(1, -1)

    xp2 = xp.reshape(bp * lp, D_MODEL)
    q, k, v, z, g = _gla_proj(xp2, *weights)
    o, s_p = _gla_prompt(q.reshape(bp, lp, kd), k.reshape(bp, lp, kd), v.reshape(bp, lp, vd),
                         g.reshape(bp, lp, kd))
    yp = _out_stage(o.reshape(bp * lp, vd), z, xp2, gh, w_out_b, gpost, GLA_DV).reshape(bp, lp, D_MODEL)

    xs2 = xs.reshape(bs * ls, D_MODEL)
    q, k, v, z, g = _gla_proj(xs2, *weights)
    o, s_s = _gla_sample(q.reshape(bs, ls, kd), k.reshape(bs, ls, kd), v.reshape(bs, ls, vd),
                         g.reshape(bs, ls, kd), state_gla, j)
    ys = _out_stage(o.reshape(bs * ls, vd), z, xs2, gh, w_out_b, gpost, GLA_DV).reshape(bs, ls, D_MODEL)
    return yp, ys, s_p, s_s


def _fox_layer(xp, xs, cache_k, cache_v, cache_lft, page_table, j, gpre, gpost, w_in, b_f, g_q, g_k,
               g_o, w_out):
    bp, lp, _ = xp.shape
    bs, ls, _ = xs.shape
    w = FOX_WIDTH
    wb = w_in.astype(BF16)
    wf = wb[:, 4 * w:]
    weights = (gpre.reshape(1, -1), wb[:, :w], wb[:, w:2 * w], wb[:, 2 * w:3 * w], wb[:, 3 * w:4 * w],
               wf, wf.T, b_f.reshape(1, -1), b_f.reshape(-1, 1),
               jnp.tile(g_q, FOX_HEADS).reshape(1, -1), jnp.tile(g_k, FOX_HEADS).reshape(1, -1))
    gh = jnp.tile(g_o, FOX_HEADS).reshape(1, -1)
    w_out_b = w_out.astype(BF16)
    gpost = gpost.reshape(1, -1)

    xp2 = xp.reshape(bp * lp, D_MODEL)
    q, k, v, z, lf, lft = _fox_proj(xp2, *weights)
    c, crow = _fox_cumsum(lf.reshape(bp, lp, FOX_HEADS), lft)
    o = _fox_flash(q.reshape(bp, lp, w), k.reshape(bp, lp, w), v.reshape(bp, lp, w), c, crow)
    yp = _out_stage(o.reshape(bp * lp, w), z, xp2, gh, w_out_b, gpost, FOX_HD).reshape(bp, lp, D_MODEL)
    new_p = (k.reshape(bp, lp, FOX_HEADS, FOX_HD), v.reshape(bp, lp, FOX_HEADS, FOX_HD),
             lf.reshape(bp, lp, FOX_HEADS))

    xs2 = xs.reshape(bs * ls, D_MODEL)
    q, k, v, z, lf, lft = _fox_proj(xs2, *weights)
    lfn_t = jnp.pad(lft.reshape(FOX_HEADS, bs, ls).transpose(1, 0, 2), ((0, 0), (0, 0), (0, LANES - ls)))
    o = _fox_decode(q.reshape(bs, ls, w), k.reshape(bs, ls, w), v.reshape(bs, ls, w), lfn_t,
                    cache_k, cache_v, cache_lft, page_table, j)
    ys = _out_stage(o.reshape(bs * ls, w), z, xs2, gh, w_out_b, gpost, FOX_HD).reshape(bs, ls, D_MODEL)
    new_s = (k.reshape(bs, ls, FOX_HEADS, FOX_HD), v.reshape(bs, ls, FOX_HEADS, FOX_HD),
             lf.reshape(bs, ls, FOX_HEADS))
    return yp, ys, new_p, new_s


def kernel(x_prompt, x_sample, state_gla, cache_k, cache_v, cache_logf, page_table, norm_pre, norm_post,
           gla_w_in, gla_w_gate2, gla_b_gate, gla_norm_o, gla_w_out, fox_w_in, fox_b_f, fox_norm_q,
           fox_norm_k, fox_norm_o, fox_w_out):
    depth = norm_pre.shape[0]
    n_fox, n_phys = cache_k.shape[:2]
    ck = cache_k.reshape(n_fox, n_phys, PAGE_SIZE, FOX_WIDTH)
    cv = cache_v.reshape(n_fox, n_phys, PAGE_SIZE, FOX_WIDTH)
    clft = cache_logf.transpose(0, 1, 3, 2)

    xp, xs = x_prompt, x_sample
    gla_p, gla_s, fox_p, fox_s = [], [], [], []
    for i in range(depth):
        j = i // 2
        if i % 2 == 0:
            xp, xs, s_p, s_s = _gla_layer(xp, xs, state_gla, j, norm_pre[i], norm_post[i], gla_w_in[j],
                                          gla_w_gate2[j], gla_b_gate[j], gla_norm_o[j], gla_w_out[j])
            gla_p.append(s_p)
            gla_s.append(s_s)
        else:
            xp, xs, new_p, new_s = _fox_layer(xp, xs, ck, cv, clft, page_table, j, norm_pre[i],
                                              norm_post[i], fox_w_in[j], fox_b_f[j], fox_norm_q[j],
                                              fox_norm_k[j], fox_norm_o[j], fox_w_out[j])
            fox_p.append(new_p)
            fox_s.append(new_s)
    kp, vp, fp = (jnp.stack(a) for a in zip(*fox_p))
    ks, vs, fs = (jnp.stack(a) for a in zip(*fox_s))
    return (xp, xs, jnp.stack(gla_p), jnp.stack(gla_s), kp, vp, fp, ks, vs, fs)
```

```python
import functools

import numpy as np
import jax
import jax.numpy as jnp
from jax import lax
from jax.experimental import pallas as pl
from jax.experimental.pallas import tpu as pltpu

F32 = jnp.float32
BF16 = jnp.bfloat16
EPS = 1e-6

D_MODEL = 1024
GLA_HEADS = 4
GLA_DK = 128
GLA_DV = 256
GLA_KDIM = GLA_HEADS * GLA_DK
GLA_VDIM = GLA_HEADS * GLA_DV
GLA_RANK = 16
GLA_TAU = 16.0
FOX_HEADS = 8
FOX_HD = 128
FOX_WIDTH = FOX_HEADS * FOX_HD
PAGE_SIZE = 128

V7X_VMEM_LIMIT_BYTES = 56 * 1024 * 1024
LANES = 128

LOG2E = 1.4426950408889634
ROWS_PER_PAGE = PAGE_SIZE * FOX_HEADS // LANES

ROW_TILE = 512
GLA_CHUNK = 128
GLA_HALF = GLA_CHUNK // 2
GLA_BLOCK = 512
ATT_BLOCK = 512


def _params(n_axes):
    return pltpu.CompilerParams(
        dimension_semantics=("arbitrary",) * n_axes,
        vmem_limit_bytes=V7X_VMEM_LIMIT_BYTES,
    )


def _dot(a, b):
    return jnp.dot(a, b, preferred_element_type=F32)


def _dot_nt(a, b):
    return lax.dot_general(a, b, (((1,), (1,)), ((), ())), preferred_element_type=F32)


def _dot_tn(a, b):
    return lax.dot_general(a, b, (((0,), (0,)), ((), ())), preferred_element_type=F32)


def _dot_exact(a, b):
    return jnp.dot(a, b, precision=lax.Precision.HIGHEST, preferred_element_type=F32)


def _log_sigmoid(x):
    return jnp.minimum(x, 0.0) - jnp.log1p(jnp.exp(-jnp.abs(x)))


def _silu(x):
    return x * (1.0 / (1.0 + jnp.exp(-x)))


def _rms_rows(x):
    return x * lax.rsqrt(jnp.mean(x * x, axis=-1, keepdims=True) + EPS)


def _group_rmsnorm(y, gain_row, group):
    outs = []
    for g0 in range(0, y.shape[-1], group):
        yg = y[:, g0:g0 + group]
        outs.append(_rms_rows(yg) * gain_row[:, g0:g0 + group])
    return jnp.concatenate(outs, axis=-1)


def _tri(n, lower):
    r = lax.broadcasted_iota(jnp.int32, (n, n), 0)
    c = lax.broadcasted_iota(jnp.int32, (n, n), 1)
    return (r >= c) if lower else (r <= c)


def _full(shape):
    nd = len(shape)
    return pl.BlockSpec(shape, lambda *_: (0,) * nd)


def _gla_proj_kernel(x_ref, gpre_ref, wq_ref, wk_ref, wv_ref, wz_ref, wlr_ref, wg2_ref, bg_ref,
                     q_ref, k_ref, v_ref, z_ref, g_ref):
    hb = (_rms_rows(x_ref[...]) * gpre_ref[...]).astype(BF16)
    q_ref[...] = _dot(hb, wq_ref[...]) * (GLA_DK ** -0.5)
    k_ref[...] = _dot(hb, wk_ref[...])
    v_ref[...] = _dot(hb, wv_ref[...])
    z_ref[...] = _dot(hb, wz_ref[...])
    lr = _dot(hb, wlr_ref[...])
    xg = _dot(lr.astype(BF16), wg2_ref[...]) + bg_ref[...]
    g_ref[...] = _log_sigmoid(xg) * (1.0 / GLA_TAU)


def _gla_proj(x, gpre, wq, wk, wv, wz, wlr, wg2, bg):
    n = x.shape[0]
    tm = min(ROW_TILE, n)
    row = lambda w: pl.BlockSpec((tm, w), lambda i: (i, 0))
    return pl.pallas_call(
        _gla_proj_kernel,
        grid=(n // tm,),
        in_specs=[row(D_MODEL), _full(gpre.shape), _full(wq.shape), _full(wk.shape), _full(wv.shape),
                  _full(wz.shape), _full(wlr.shape), _full(wg2.shape), _full(bg.shape)],
        out_specs=[row(GLA_KDIM), row(GLA_KDIM), row(GLA_VDIM), row(GLA_VDIM), row(GLA_KDIM)],
        out_shape=[jax.ShapeDtypeStruct((n, GLA_KDIM), F32), jax.ShapeDtypeStruct((n, GLA_KDIM), F32),
                   jax.ShapeDtypeStruct((n, GLA_VDIM), F32), jax.ShapeDtypeStruct((n, GLA_VDIM), F32),
                   jax.ShapeDtypeStruct((n, GLA_KDIM), F32)],
        compiler_params=_params(1),
        name="gla_proj",
    )(x, gpre, wq, wk, wv, wz, wlr, wg2, bg)


def _gla_intra_scores(q, k, b):
    h = GLA_HALF
    q0, q1 = q[:h], q[h:]
    k0, k1 = k[:h], k[h:]
    b0, b1 = b[:h], b[h:]
    r0 = b[h // 2 - 1:h // 2]
    rb = b[h - 1:h]
    r1 = b[h + h // 2 - 1:h + h // 2]
    tril = _tri(h, lower=True)
    a00 = _dot_nt((q0 * jnp.exp(b0 - r0)).astype(BF16), (k0 * jnp.exp(r0 - b0)).astype(BF16))
    a11 = _dot_nt((q1 * jnp.exp(b1 - r1)).astype(BF16), (k1 * jnp.exp(r1 - b1)).astype(BF16))
    a10 = _dot_nt((q1 * jnp.exp(b1 - rb)).astype(BF16), (k0 * jnp.exp(rb - b0)).astype(BF16))
    a00 = jnp.where(tril, a00, 0.0)
    a11 = jnp.where(tril, a11, 0.0)
    top = jnp.concatenate([a00, jnp.zeros_like(a00)], axis=1)
    bot = jnp.concatenate([a10, a11], axis=1)
    return jnp.concatenate([top, bot], axis=0)


def _gla_prompt_kernel(q_ref, k_ref, v_ref, g_ref, o_ref, s_ref):
    @pl.when(pl.program_id(1) == 0)
    def _():
        s_ref[...] = jnp.zeros_like(s_ref)

    c = GLA_CHUNK
    tril = _tri(c, lower=True).astype(F32)
    for ci in range(GLA_BLOCK // c):
        rows = slice(ci * c, (ci + 1) * c)
        for h in range(GLA_HEADS):
            kcol = slice(h * GLA_DK, (h + 1) * GLA_DK)
            vcol = slice(h * GLA_DV, (h + 1) * GLA_DV)
            q = q_ref[0, rows, kcol]
            k = k_ref[0, rows, kcol]
            v = v_ref[0, rows, vcol].astype(BF16)
            b = _dot_exact(tril, g_ref[0, rows, kcol])
            b_last = b[c - 1:c]
            s_old = s_ref[0, h]
            o_inter = _dot((q * jnp.exp(b)).astype(BF16), s_old.astype(BF16))
            a = _gla_intra_scores(q, k, b)
            o_ref[0, rows, vcol] = o_inter + _dot(a.astype(BF16), v)
            k_dec = (k * jnp.exp(b_last - b)).astype(BF16)
            e_col = jnp.broadcast_to(jnp.exp(b_last), (GLA_DK, GLA_DK)).T
            e_col = jnp.concatenate([e_col, e_col], axis=1)
            s_ref[0, h] = e_col * s_old + _dot_tn(k_dec, v)


def _gla_prompt(q, k, v, g):
    b, l, _ = q.shape
    blk = lambda w: pl.BlockSpec((1, GLA_BLOCK, w), lambda bi, i: (bi, i, 0))
    return pl.pallas_call(
        _gla_prompt_kernel,
        grid=(b, l // GLA_BLOCK),
        in_specs=[blk(GLA_KDIM), blk(GLA_KDIM), blk(GLA_VDIM), blk(GLA_KDIM)],
        out_specs=[blk(GLA_VDIM),
                   pl.BlockSpec((1, GLA_HEADS, GLA_DK, GLA_DV), lambda bi, i: (bi, 0, 0, 0))],
        out_shape=[jax.ShapeDtypeStruct((b, l, GLA_VDIM), F32),
                   jax.ShapeDtypeStruct((b, GLA_HEADS, GLA_DK, GLA_DV), F32)],
        compiler_params=_params(2),
        name="gla_prompt",
    )(q, k, v, g)


def _gla_sample_kernel(q_ref, k_ref, v_ref, g_ref, s0_ref, *refs):
    o_ref, s_ref = refs[-2:]
    t = q_ref.shape[1]
    tril = _tri(t, lower=True)
    g = g_ref[0]
    b = _dot_exact(tril.astype(F32), g)
    b_last = b[t - 1:t]
    q = q_ref[0]
    k = k_ref[0]
    qe = q * jnp.exp(b)
    ke = k * jnp.exp(-b)
    k_dec = k * jnp.exp(b_last - b)
    e_last = jnp.exp(b_last)
    for h in range(GLA_HEADS):
        kcol = slice(h * GLA_DK, (h + 1) * GLA_DK)
        vcol = slice(h * GLA_DV, (h + 1) * GLA_DV)
        v = v_ref[0, :, vcol].astype(BF16)
        s_old = s0_ref[0, 0, h]
        a = _dot_nt(qe[:, kcol].astype(BF16), ke[:, kcol].astype(BF16))
        a = jnp.where(tril, a, 0.0)
        o_ref[0, :, vcol] = _dot(qe[:, kcol].astype(BF16), s_old.astype(BF16)) + _dot(a.astype(BF16), v)
        e_col = jnp.broadcast_to(e_last[:, kcol], (GLA_DK, GLA_DK)).T
        e_col = jnp.concatenate([e_col, e_col], axis=1)
        s_ref[0, 0, h] = e_col * s_old + _dot_tn(k_dec[:, kcol].astype(BF16), v)


def _gla_sample(q, k, v, g, state, j, new_state=None):
    n, t, _ = q.shape
    blk = lambda w: pl.BlockSpec((1, t, w), lambda i: (i, 0, 0))
    state_blk = pl.BlockSpec((1, 1, GLA_HEADS, GLA_DK, GLA_DV), lambda i: (j, i, 0, 0, 0))
    in_specs = [blk(GLA_KDIM), blk(GLA_KDIM), blk(GLA_VDIM), blk(GLA_KDIM), state_blk]
    args = [q, k, v, g, state]
    aliases = {}
    if new_state is not None:
        in_specs.append(pl.BlockSpec(memory_space=pl.ANY))
        args.append(new_state)
        aliases = {len(args) - 1: 1}
    return pl.pallas_call(
        _gla_sample_kernel,
        grid=(n,),
        in_specs=in_specs,
        out_specs=[blk(GLA_VDIM), state_blk],
        out_shape=[jax.ShapeDtypeStruct((n, t, GLA_VDIM), F32),
                   jax.ShapeDtypeStruct(state.shape, F32)],
        input_output_aliases=aliases,
        compiler_params=_params(1),
        name="gla_sample",
    )(*args)


def _store_head_major(ref, y):
    rows = y.shape[0]
    for h in range(FOX_HEADS):
        ref[0, pl.ds(h, rows, stride=FOX_HEADS), :] = y[:, h * FOX_HD:(h + 1) * FOX_HD]


def _fox_proj_prompt_kernel(x_ref, gpre_ref, wq_ref, wk_ref, wv_ref, wz_ref, wf_ref, wft_ref, bf_ref,
                            bft_ref, gq_ref, gk_ref, *refs):
    kf_ref, vf_ref, qb_ref, kb_ref, vb_ref, z_ref, lf_ref, lft_ref = refs[-8:]
    hb = (_rms_rows(x_ref[...]) * gpre_ref[...]).astype(BF16)
    q = _group_rmsnorm(_dot(hb, wq_ref[...]), gq_ref[...], FOX_HD)
    qb_ref[...] = (q * (FOX_HD ** -0.5 * LOG2E)).astype(BF16)
    k = _group_rmsnorm(_dot(hb, wk_ref[...]), gk_ref[...], FOX_HD)
    kb_ref[...] = k.astype(BF16)
    _store_head_major(kf_ref, k)
    v = _dot(hb, wv_ref[...])
    vb_ref[...] = v.astype(BF16)
    _store_head_major(vf_ref, v)
    z_ref[...] = _dot(hb, wz_ref[...])
    lf_ref[...] = _log_sigmoid(_dot(hb, wf_ref[...]) + bf_ref[...])
    lft_ref[...] = _log_sigmoid(_dot_nt(wft_ref[...], hb) + bft_ref[...])


def _fox_proj_prompt(x, weights, j, n_layers, stacked=None):
    n = x.shape[0]
    tm = min(ROW_TILE, n)
    row = lambda w: pl.BlockSpec((tm, w), lambda i: (i, 0))
    fin = pl.BlockSpec((1, tm * FOX_HEADS, FOX_HD), lambda i: (j, i, 0))
    fin_shape = jax.ShapeDtypeStruct((n_layers, n * FOX_HEADS, FOX_HD), F32)
    wide_b = jax.ShapeDtypeStruct((n, FOX_WIDTH), BF16)
    in_specs = [row(D_MODEL)] + [_full(w.shape) for w in weights]
    args = [x, *weights]
    aliases = {}
    if stacked is not None:
        in_specs += [pl.BlockSpec(memory_space=pl.ANY)] * 2
        aliases = {len(args): 0, len(args) + 1: 1}
        args += list(stacked)
    return pl.pallas_call(
        _fox_proj_prompt_kernel,
        grid=(n // tm,),
        in_specs=in_specs,
        out_specs=[fin, fin, row(FOX_WIDTH), row(FOX_WIDTH), row(FOX_WIDTH), row(FOX_WIDTH),
                   row(FOX_HEADS), pl.BlockSpec((FOX_HEADS, tm), lambda i: (0, i))],
        out_shape=[fin_shape, fin_shape, wide_b, wide_b, wide_b, jax.ShapeDtypeStruct((n, FOX_WIDTH), F32),
                   jax.ShapeDtypeStruct((n, FOX_HEADS), F32), jax.ShapeDtypeStruct((FOX_HEADS, n), F32)],
        input_output_aliases=aliases,
        compiler_params=_params(1),
        name="fox_proj_prompt",
    )(*args)


def _fox_proj_sample_kernel(x_ref, gpre_ref, wq_ref, wk_ref, wv_ref, wz_ref, wf_ref, wft_ref, bf_ref,
                            bft_ref, gq_ref, gk_ref, q_ref, k_ref, v_ref, z_ref, lf_ref):
    del wft_ref, bft_ref
    hb = (_rms_rows(x_ref[...]) * gpre_ref[...]).astype(BF16)
    _store_head_major(q_ref, _group_rmsnorm(_dot(hb, wq_ref[...]), gq_ref[...], FOX_HD))
    _store_head_major(k_ref, _group_rmsnorm(_dot(hb, wk_ref[...]), gk_ref[...], FOX_HD))
    _store_head_major(v_ref, _dot(hb, wv_ref[...]))
    z_ref[...] = _dot(hb, wz_ref[...])
    lf_ref[...] = _log_sigmoid(_dot(hb, wf_ref[...]) + bf_ref[...])


def _fox_proj_sample(x, weights):
    n = x.shape[0]
    tm = min(ROW_TILE, n)
    row = lambda w: pl.BlockSpec((tm, w), lambda i: (i, 0))
    fin = pl.BlockSpec((1, tm * FOX_HEADS, FOX_HD), lambda i: (0, i, 0))
    fin_shape = jax.ShapeDtypeStruct((1, n * FOX_HEADS, FOX_HD), F32)
    return pl.pallas_call(
        _fox_proj_sample_kernel,
        grid=(n // tm,),
        in_specs=[row(D_MODEL)] + [_full(w.shape) for w in weights],
        out_specs=[fin, fin, fin, row(FOX_WIDTH), row(FOX_HEADS)],
        out_shape=[fin_shape, fin_shape, fin_shape, jax.ShapeDtypeStruct((n, FOX_WIDTH), F32),
                   jax.ShapeDtypeStruct((n, FOX_HEADS), F32)],
        compiler_params=_params(1),
        name="fox_proj_sample",
    )(x, *weights)


def _fox_cumsum_kernel(lf_ref, lft_ref, c_ref, crow_ref):
    l = lf_ref.shape[1]
    tril = _tri(LANES, lower=True).astype(F32)
    triu = _tri(LANES, lower=False).astype(F32)
    carry_row = jnp.zeros((1, FOX_HEADS), F32)
    carry_col = jnp.zeros((FOX_HEADS, 1), F32)
    for i in range(l // LANES):
        sl = slice(i * LANES, (i + 1) * LANES)
        c = _dot_exact(tril, lf_ref[0, sl, :]) + carry_row
        carry_row = c[LANES - 1:LANES, :]
        c_ref[0, sl, :] = c
        ct = _dot_exact(lft_ref[:, sl], triu) + carry_col
        carry_col = ct[:, LANES - 1:LANES]
        for h in range(FOX_HEADS):
            crow_ref[0, h, :, sl] = ct[h:h + 1, :]


def _fox_cumsum(lf, lft):
    b, l, _ = lf.shape
    return pl.pallas_call(
        _fox_cumsum_kernel,
        grid=(b,),
        in_specs=[pl.BlockSpec((1, l, FOX_HEADS), lambda i: (i, 0, 0)),
                  pl.BlockSpec((FOX_HEADS, l), lambda i: (0, i))],
        out_specs=[pl.BlockSpec((1, l, FOX_HEADS), lambda i: (i, 0, 0)),
                   pl.BlockSpec((1, FOX_HEADS, 1, l), lambda i: (i, 0, 0, 0))],
        out_shape=[jax.ShapeDtypeStruct((b, l, FOX_HEADS), F32),
                   jax.ShapeDtypeStruct((b, FOX_HEADS, 1, l), F32)],
        compiler_params=_params(1),
        name="fox_cumsum",
    )(lf, lft)


def _fox_flash_kernel(q_ref, k_ref, v_ref, c_ref, crow_ref, o_ref, ck_ref, m_ref, l_ref, acc_ref):
    h = pl.program_id(1)
    l = q_ref.shape[1]
    t = ATT_BLOCK
    nb = l // t
    lane = lax.broadcasted_iota(jnp.int32, (l, FOX_HEADS), 1)
    col = jnp.sum(jnp.where(lane == h, c_ref[0], 0.0), axis=-1, keepdims=True) * LOG2E
    ck_ref[...] = jnp.broadcast_to(col, (l, LANES))
    m_ref[...] = jnp.full_like(m_ref, -jnp.inf)
    l_ref[...] = jnp.zeros_like(l_ref)
    acc_ref[...] = jnp.zeros_like(acc_ref)
    causal = _tri(t, lower=False)

    def block(kb, vt, ck, qb, masked):
        st = _dot_nt(kb, q_ref[0, qb * t:(qb + 1) * t, :]) - ck
        if masked:
            st = jnp.where(causal, st, -jnp.inf)
        cq = crow_ref[0, 0, qb:qb + 1, :] * LOG2E
        m_old = m_ref[qb]
        m_new = jnp.maximum(m_old, jnp.max(st, axis=0, keepdims=True) + cq)
        alpha = jnp.exp2(m_old - m_new)
        p = jnp.exp2(st - (m_new - cq))
        l_ref[qb] = alpha * l_ref[qb] + jnp.sum(p, axis=0, keepdims=True)
        acc_ref[qb] = alpha * acc_ref[qb] + _dot(vt, p.astype(BF16))
        m_ref[qb] = m_new

    for ki in range(nb):
        ks = slice(ki * t, (ki + 1) * t)
        kb = k_ref[0, ks, :]
        vt = v_ref[0, ks, :].astype(F32).T.astype(BF16)
        ck = jnp.tile(ck_ref[ks, :], (1, t // LANES))
        block(kb, vt, ck, ki, True)
        for qb in range(ki + 1, nb):
            block(kb, vt, ck, qb, False)

    for qb in range(nb):
        o_ref[0, qb * t:(qb + 1) * t, :] = (acc_ref[qb] / l_ref[qb]).T


def _fox_flash(q, k, v, c, crow):
    b, l, _ = q.shape
    t = ATT_BLOCK
    nb = l // t
    head = pl.BlockSpec((1, l, FOX_HD), lambda bi, h: (bi, 0, h))
    return pl.pallas_call(
        _fox_flash_kernel,
        grid=(b, FOX_HEADS),
        in_specs=[head, head, head,
                  pl.BlockSpec((1, l, FOX_HEADS), lambda bi, h: (bi, 0, 0)),
                  pl.BlockSpec((1, 1, nb, t), lambda bi, h: (bi, h, 0, 0))],
        out_specs=head,
        out_shape=jax.ShapeDtypeStruct((b, l, FOX_WIDTH), F32),
        scratch_shapes=[pltpu.VMEM((l, LANES), F32), pltpu.VMEM((nb, 1, t), F32),
                        pltpu.VMEM((nb, 1, t), F32), pltpu.VMEM((nb, FOX_HD, t), F32)],
        compiler_params=_params(2),
        name="fox_flash",
    )(q, k, v, c, crow.reshape(b, FOX_HEADS, nb, t))


def _fox_decode_kernel(n_pages, pt_ref, q_ref, kn_ref, vn_ref, lfn_ref, *refs):
    del pt_ref
    kp_refs = refs[:n_pages]
    vp_refs = refs[n_pages:2 * n_pages]
    lf_refs = refs[2 * n_pages:3 * n_pages]
    o_ref = refs[3 * n_pages]
    nq = q_ref.shape[1]
    n_rows = n_pages * ROWS_PER_PAGE

    li = lax.broadcasted_iota(jnp.int32, (LANES, LANES), 0)
    lj = lax.broadcasted_iota(jnp.int32, (LANES, LANES), 1)
    same_head = (li % FOX_HEADS) == (lj % FOX_HEADS)
    m_incl = (same_head & (li <= lj)).astype(F32)
    m_all = same_head.astype(F32)
    strict = (lax.broadcasted_iota(jnp.int32, (n_rows, n_rows), 0)
              > lax.broadcasted_iota(jnp.int32, (n_rows, n_rows), 1)).astype(F32)

    x = jnp.concatenate([lf_refs[p][0, 0] for p in range(n_pages)], axis=0)
    row_tot = _dot_exact(x, m_all)
    cp = _dot_exact(x, m_incl) + _dot_exact(strict, row_tot)
    past_bias = jnp.sum(row_tot, axis=0, keepdims=True) - cp
    cn_row = _dot_exact(lfn_ref[0], m_incl)
    qi = lax.broadcasted_iota(jnp.int32, (nq, LANES), 0)
    qj = lax.broadcasted_iota(jnp.int32, (nq, LANES), 1)
    cn_col = jnp.sum(jnp.where(qi == qj, jnp.broadcast_to(cn_row, (nq, LANES)), 0.0), axis=1, keepdims=True)

    head_ok = (qi % FOX_HEADS) == (qj % FOX_HEADS)
    qb = (q_ref[0] * (FOX_HD ** -0.5)).astype(BF16)

    chunks = []
    for p in range(n_pages):
        s = _dot_nt(qb, kp_refs[p][0, 0].astype(BF16))
        for r in range(ROWS_PER_PAGE):
            i = p * ROWS_PER_PAGE + r
            lg = s[:, r * LANES:(r + 1) * LANES] + past_bias[i:i + 1, :] + cn_col
            chunks.append(jnp.where(head_ok, lg, -jnp.inf))
    s_new = _dot_nt(qb, kn_ref[0].astype(BF16))
    ni = lax.broadcasted_iota(jnp.int32, (nq, nq), 0)
    nj = lax.broadcasted_iota(jnp.int32, (nq, nq), 1)
    causal = ((ni % FOX_HEADS) == (nj % FOX_HEADS)) & ((nj // FOX_HEADS) <= (ni // FOX_HEADS))
    l_new = jnp.where(causal, s_new + cn_col - cn_row[:, :nq], -jnp.inf)

    m = jnp.max(l_new, axis=-1, keepdims=True)
    mx = chunks[0]
    for c in chunks[1:]:
        mx = jnp.maximum(mx, c)
    m = jnp.maximum(m, jnp.max(mx, axis=-1, keepdims=True))
    p_new = jnp.exp(l_new - m)
    acc = _dot(p_new.astype(BF16), vn_ref[0].astype(BF16))
    den = jnp.zeros((nq, LANES), F32)
    for p in range(n_pages):
        ps = [jnp.exp(chunks[p * ROWS_PER_PAGE + r] - m) for r in range(ROWS_PER_PAGE)]
        for e in ps:
            den = den + e
        acc = acc + _dot(jnp.concatenate(ps, axis=1).astype(BF16), vp_refs[p][0, 0].astype(BF16))
    denom = jnp.sum(den, axis=-1, keepdims=True) + jnp.sum(p_new, axis=-1, keepdims=True)
    o_ref[0] = acc / denom


def _fox_decode(q, kn, vn, lfn, cache_k, cache_v, cache_lf, page_table, j):
    n, nq, _ = q.shape
    n_pages = page_table.shape[1]
    tok = pl.BlockSpec((1, nq, FOX_HD), lambda i, pt: (i, 0, 0))

    def page_spec(p, shape):
        return pl.BlockSpec((1, 1) + shape, lambda i, pt: (j, pt[i * n_pages + p], 0, 0))

    in_specs = ([tok, tok, tok, pl.BlockSpec((1, 1, LANES), lambda i, pt: (i, 0, 0))]
                + [page_spec(p, (PAGE_SIZE * FOX_HEADS, FOX_HD)) for p in range(n_pages)]
                + [page_spec(p, (PAGE_SIZE * FOX_HEADS, FOX_HD)) for p in range(n_pages)]
                + [page_spec(p, (ROWS_PER_PAGE, LANES)) for p in range(n_pages)])
    grid_spec = pltpu.PrefetchScalarGridSpec(
        num_scalar_prefetch=1, grid=(n,), in_specs=in_specs, out_specs=tok)
    return pl.pallas_call(
        functools.partial(_fox_decode_kernel, n_pages),
        grid_spec=grid_spec,
        out_shape=jax.ShapeDtypeStruct((n, nq, FOX_HD), F32),
        compiler_params=_params(1),
        name="fox_decode",
    )(page_table.reshape(-1), q, kn, vn, lfn,
      *([cache_k] * n_pages), *([cache_v] * n_pages), *([cache_lf] * n_pages))


def _out_kernel(group, o_ref, z_ref, x_ref, gh_ref, w_ref, gpost_ref, y_ref):
    gated = _group_rmsnorm(o_ref[...], gh_ref[...], group) * _silu(z_ref[...])
    y = _dot(gated.astype(BF16), w_ref[...])
    y_ref[...] = x_ref[...] + _rms_rows(y) * gpost_ref[...]


def _out_stage(o, z, x, gh, w, gpost, group):
    n = x.shape[0]
    tm = min(ROW_TILE, n)
    row = pl.BlockSpec((tm, D_MODEL), lambda i: (i, 0))
    return pl.pallas_call(
        functools.partial(_out_kernel, group),
        grid=(n // tm,),
        in_specs=[row, row, row, _full(gh.shape), _full(w.shape), _full(gpost.shape)],
        out_specs=row,
        out_shape=jax.ShapeDtypeStruct((n, D_MODEL), F32),
        compiler_params=_params(1),
        name="out_stage",
    )(o, z, x, gh, w, gpost)


def _gla_layer(xp, xs, state_gla, new_state, j, gpre, gpost, w_in, w_gate2, b_gate, g_out, w_out):
    bp, lp, _ = xp.shape
    bs, ls, _ = xs.shape
    kd, vd = GLA_KDIM, GLA_VDIM
    wb = w_in.astype(BF16)
    weights = (gpre.reshape(1, -1), wb[:, :kd], wb[:, kd:2 * kd], wb[:, 2 * kd:2 * kd + vd],
               wb[:, 2 * kd + vd:2 * kd + 2 * vd], wb[:, 2 * kd + 2 * vd:], w_gate2.astype(BF16),
               b_gate.reshape(1, -1))
    gh = jnp.tile(g_out, GLA_HEADS).reshape(1, -1)
    w_out_b = w_out.astype(BF16)
    gpost = gpost.reshape(1, -1)

    xp2 = xp.reshape(bp * lp, D_MODEL)
    q, k, v, z, g = _gla_proj(xp2, *weights)
    o, s_p = _gla_prompt(q.reshape(bp, lp, kd), k.reshape(bp, lp, kd), v.reshape(bp, lp, vd),
                         g.reshape(bp, lp, kd))
    yp = _out_stage(o.reshape(bp * lp, vd), z, xp2, gh, w_out_b, gpost, GLA_DV).reshape(bp, lp, D_MODEL)

    xs2 = xs.reshape(bs * ls, D_MODEL)
    q, k, v, z, g = _gla_proj(xs2, *weights)
    o, s_s = _gla_sample(q.reshape(bs, ls, kd), k.reshape(bs, ls, kd), v.reshape(bs, ls, vd),
                         g.reshape(bs, ls, kd), state_gla, j, new_state)
    ys = _out_stage(o.reshape(bs * ls, vd), z, xs2, gh, w_out_b, gpost, GLA_DV).reshape(bs, ls, D_MODEL)
    return yp, ys, s_p, s_s


def _fox_layer(xp, xs, cache_k, cache_v, cache_lf, page_table, kv_stacked, j, n_layers, gpre, gpost, w_in,
               b_f, g_q, g_k, g_o, w_out):
    bp, lp, _ = xp.shape
    bs, ls, _ = xs.shape
    w = FOX_WIDTH
    wb = w_in.astype(BF16)
    wf = wb[:, 4 * w:]
    weights = (gpre.reshape(1, -1), wb[:, :w], wb[:, w:2 * w], wb[:, 2 * w:3 * w], wb[:, 3 * w:4 * w],
               wf, wf.T, b_f.reshape(1, -1), b_f.reshape(-1, 1),
               jnp.tile(g_q, FOX_HEADS).reshape(1, -1), jnp.tile(g_k, FOX_HEADS).reshape(1, -1))
    gh = jnp.tile(g_o, FOX_HEADS).reshape(1, -1)
    w_out_b = w_out.astype(BF16)
    gpost = gpost.reshape(1, -1)

    xp2 = xp.reshape(bp * lp, D_MODEL)
    kf, vf, qb, kb, vb, z, lf, lft = _fox_proj_prompt(xp2, weights, j, n_layers, kv_stacked)
    c, crow = _fox_cumsum(lf.reshape(bp, lp, FOX_HEADS), lft)
    o = _fox_flash(qb.reshape(bp, lp, w), kb.reshape(bp, lp, w), vb.reshape(bp, lp, w), c, crow)
    yp = _out_stage(o.reshape(bp * lp, w), z, xp2, gh, w_out_b, gpost, FOX_HD).reshape(bp, lp, D_MODEL)

    xs2 = xs.reshape(bs * ls, D_MODEL)
    nq = ls * FOX_HEADS
    q, k, v, z, lfs = _fox_proj_sample(xs2, weights)
    lfn = jnp.pad(lfs.reshape(bs, 1, nq), ((0, 0), (0, 0), (0, LANES - nq)))
    o = _fox_decode(q.reshape(bs, nq, FOX_HD), k.reshape(bs, nq, FOX_HD), v.reshape(bs, nq, FOX_HD), lfn,
                    cache_k, cache_v, cache_lf, page_table, j)
    ys = _out_stage(o.reshape(bs * ls, w), z, xs2, gh, w_out_b, gpost, FOX_HD).reshape(bs, ls, D_MODEL)
    new_s = (k.reshape(bs, ls, FOX_HEADS, FOX_HD), v.reshape(bs, ls, FOX_HEADS, FOX_HD),
             lfs.reshape(bs, ls, FOX_HEADS))
    return yp, ys, (kf, vf), lf.reshape(bp, lp, FOX_HEADS), new_s


def kernel(x_prompt, x_sample, state_gla, cache_k, cache_v, cache_logf, page_table, norm_pre, norm_post,
           gla_w_in, gla_w_gate2, gla_b_gate, gla_norm_o, gla_w_out, fox_w_in, fox_b_f, fox_norm_q,
           fox_norm_k, fox_norm_o, fox_w_out):
    depth = norm_pre.shape[0]
    bp, lp, _ = x_prompt.shape
    n_fox, n_phys = cache_k.shape[:2]
    ck = cache_k.reshape(n_fox, n_phys, PAGE_SIZE * FOX_HEADS, FOX_HD)
    cv = cache_v.reshape(n_fox, n_phys, PAGE_SIZE * FOX_HEADS, FOX_HD)
    clf = cache_logf.reshape(n_fox, n_phys, ROWS_PER_PAGE, LANES)

    xp, xs = x_prompt, x_sample
    gla_p, fox_lf, fox_s = [], [], []
    gla_s = None
    kv_stacked = None
    for i in range(depth):
        j = i // 2
        if i % 2 == 0:
            xp, xs, s_p, gla_s = _gla_layer(xp, xs, state_gla, gla_s, j, norm_pre[i], norm_post[i],
                                            gla_w_in[j], gla_w_gate2[j], gla_b_gate[j], gla_norm_o[j],
                                            gla_w_out[j])
            gla_p.append(s_p)
        else:
            xp, xs, kv_stacked, lf, new_s = _fox_layer(xp, xs, ck, cv, clf, page_table, kv_stacked, j, n_fox,
                                                       norm_pre[i], norm_post[i], fox_w_in[j], fox_b_f[j],
                                                       fox_norm_q[j], fox_norm_k[j], fox_norm_o[j],
                                                       fox_w_out[j])
            fox_lf.append(lf)
            fox_s.append(new_s)
    kp, vp = (a.reshape(n_fox, bp, lp, FOX_HEADS, FOX_HD) for a in kv_stacked)
    ks, vs, fs = (jnp.stack(a) for a in zip(*fox_s))
    return (xp, xs, jnp.stack(gla_p), gla_s, kp, vp, jnp.stack(fox_lf), ks, vs, fs)
```

```python
import functools

import numpy as np
import jax
import jax.numpy as jnp
from jax import lax
from jax.experimental import pallas as pl
from jax.experimental.pallas import tpu as pltpu

F32 = jnp.float32
BF16 = jnp.bfloat16
EPS = 1e-6

D_MODEL = 1024
GLA_HEADS = 4
GLA_DK = 128
GLA_DV = 256
GLA_KDIM = GLA_HEADS * GLA_DK
GLA_VDIM = GLA_HEADS * GLA_DV
GLA_RANK = 16
GLA_TAU = 16.0
FOX_HEADS = 8
FOX_HD = 128
FOX_WIDTH = FOX_HEADS * FOX_HD
PAGE_SIZE = 128

V7X_VMEM_LIMIT_BYTES = 56 * 1024 * 1024
LANES = 128

LOG2E = 1.4426950408889634
ROWS_PER_PAGE = PAGE_SIZE * FOX_HEADS // LANES

ROW_TILE = 512
GLA_CHUNK = 128
GLA_HALF = GLA_CHUNK // 2
GLA_BLOCK = 512
GLA_SAMPLE_GROUP = 4
ATT_BLOCK = 512


def _params(n_axes):
    return pltpu.CompilerParams(
        dimension_semantics=("arbitrary",) * n_axes,
        vmem_limit_bytes=V7X_VMEM_LIMIT_BYTES,
    )


def _dot(a, b):
    return jnp.dot(a, b, preferred_element_type=F32)


def _dot_nt(a, b):
    return lax.dot_general(a, b, (((1,), (1,)), ((), ())), preferred_element_type=F32)


def _dot_tn(a, b):
    return lax.dot_general(a, b, (((0,), (0,)), ((), ())), preferred_element_type=F32)


def _dot_exact(a, b):
    return jnp.dot(a, b, precision=lax.Precision.HIGHEST, preferred_element_type=F32)


def _log_sigmoid(x):
    return jnp.minimum(x, 0.0) - jnp.log1p(jnp.exp(-jnp.abs(x)))


def _silu(x):
    return x * (1.0 / (1.0 + jnp.exp(-x)))


def _rms_rows(x):
    return x * lax.rsqrt(jnp.mean(x * x, axis=-1, keepdims=True) + EPS)


def _group_rmsnorm(y, gain_row, group):
    outs = []
    for g0 in range(0, y.shape[-1], group):
        yg = y[:, g0:g0 + group]
        outs.append(_rms_rows(yg) * gain_row[:, g0:g0 + group])
    return jnp.concatenate(outs, axis=-1)


def _tri(n, lower):
    r = lax.broadcasted_iota(jnp.int32, (n, n), 0)
    c = lax.broadcasted_iota(jnp.int32, (n, n), 1)
    return (r >= c) if lower else (r <= c)


def _full(shape):
    nd = len(shape)
    return pl.BlockSpec(shape, lambda *_: (0,) * nd)


def _gla_proj_kernel(x_ref, gpre_ref, wq_ref, wk_ref, wv_ref, wz_ref, wlr_ref, wg2_ref, bg_ref,
                     q_ref, k_ref, v_ref, z_ref, g_ref):
    hb = (_rms_rows(x_ref[...]) * gpre_ref[...]).astype(BF16)
    q_ref[...] = (_dot(hb, wq_ref[...]) * (GLA_DK ** -0.5)).astype(BF16)
    k_ref[...] = _dot(hb, wk_ref[...]).astype(BF16)
    v_ref[...] = _dot(hb, wv_ref[...]).astype(BF16)
    z_ref[...] = _dot(hb, wz_ref[...]).astype(BF16)
    lr = _dot(hb, wlr_ref[...])
    xg = _dot(lr.astype(BF16), wg2_ref[...]) + bg_ref[...]
    g_ref[...] = _log_sigmoid(xg) * (1.0 / GLA_TAU)


def _gla_proj(x, gpre, wq, wk, wv, wz, wlr, wg2, bg):
    n = x.shape[0]
    tm = min(ROW_TILE, n)
    row = lambda w: pl.BlockSpec((tm, w), lambda i: (i, 0))
    return pl.pallas_call(
        _gla_proj_kernel,
        grid=(n // tm,),
        in_specs=[row(D_MODEL), _full(gpre.shape), _full(wq.shape), _full(wk.shape), _full(wv.shape),
                  _full(wz.shape), _full(wlr.shape), _full(wg2.shape), _full(bg.shape)],
        out_specs=[row(GLA_KDIM), row(GLA_KDIM), row(GLA_VDIM), row(GLA_VDIM), row(GLA_KDIM)],
        out_shape=[jax.ShapeDtypeStruct((n, GLA_KDIM), BF16), jax.ShapeDtypeStruct((n, GLA_KDIM), BF16),
                   jax.ShapeDtypeStruct((n, GLA_VDIM), BF16), jax.ShapeDtypeStruct((n, GLA_VDIM), BF16),
                   jax.ShapeDtypeStruct((n, GLA_KDIM), F32)],
        compiler_params=_params(1),
        name="gla_proj",
    )(x, gpre, wq, wk, wv, wz, wlr, wg2, bg)


def _split3(x):
    x1 = x.astype(BF16)
    r1 = x - x1.astype(F32)
    x2 = r1.astype(BF16)
    x3 = (r1 - x2.astype(F32)).astype(BF16)
    return x1, x2, x3


def _prefix_sum_rows(tril_b, x):
    x1, x2, x3 = _split3(x)
    return _dot(tril_b, x1) + (_dot(tril_b, x2) + _dot(tril_b, x3))


def _gla_prompt_kernel(q_ref, k_ref, v_ref, g_ref, o_ref, s_ref, u_ref, e_ref, qe_ref):
    @pl.when(pl.program_id(1) == 0)
    def _():
        s_ref[...] = jnp.zeros_like(s_ref)

    c, hf = GLA_CHUNK, GLA_HALF
    nc = GLA_BLOCK // c
    tril_b = _tri(c, lower=True).astype(BF16)
    tril_h = _tri(hf, lower=True)
    ids = [(ci, h) for ci in range(nc) for h in range(GLA_HEADS)]
    rows = lambda ci: slice(ci * c, (ci + 1) * c)
    kcol = lambda h: slice(h * GLA_DK, (h + 1) * GLA_DK)
    vcol = lambda h: slice(h * GLA_DV, (h + 1) * GLA_DV)

    bs = [_prefix_sum_rows(tril_b, g_ref[0, rows(ci), :]) for ci in range(nc)]

    ops = []
    for ci, h in ids:
        b = bs[ci][:, kcol(h)]
        q = q_ref[0, rows(ci), kcol(h)].astype(F32)
        k = k_ref[0, rows(ci), kcol(h)].astype(F32)
        b0, b1 = b[:hf], b[hf:]
        r0 = b[hf // 2 - 1:hf // 2]
        rb = b[hf - 1:hf]
        r1 = b[hf + hf // 2 - 1:hf + hf // 2]
        b_last = b[c - 1:c]
        ops.append(dict(
            qe=(q * jnp.exp(b)).astype(BF16),
            kd=(k * jnp.exp(b_last - b)).astype(BF16),
            e=jnp.exp(b_last),
            q0=(q[:hf] * jnp.exp(b0 - r0)).astype(BF16), k0=(k[:hf] * jnp.exp(r0 - b0)).astype(BF16),
            q1=(q[hf:] * jnp.exp(b1 - r1)).astype(BF16), k1=(k[hf:] * jnp.exp(r1 - b1)).astype(BF16),
            q1b=(q[hf:] * jnp.exp(b1 - rb)).astype(BF16), k0b=(k[:hf] * jnp.exp(rb - b0)).astype(BF16)))

    for d in ops:
        a00 = jnp.where(tril_h, _dot_nt(d["q0"], d["k0"]), 0.0)
        a11 = jnp.where(tril_h, _dot_nt(d["q1"], d["k1"]), 0.0)
        a10 = _dot_nt(d["q1b"], d["k0b"])
        top = jnp.concatenate([a00, jnp.zeros_like(a00)], axis=1)
        bot = jnp.concatenate([a10, a11], axis=1)
        d["a"] = jnp.concatenate([top, bot], axis=0).astype(BF16)
    for i, (ci, h) in enumerate(ids):
        d = ops[i]
        v = v_ref[0, rows(ci), vcol(h)]
        d["o"] = _dot(d["a"], v)
        u_ref[i] = _dot_tn(d["kd"], v)
        e_col = jnp.broadcast_to(d["e"], (GLA_DK, GLA_DK)).T
        e_ref[i] = jnp.concatenate([e_col, e_col], axis=1)
        qe_ref[i] = d["qe"]

    for i, (ci, h) in enumerate(ids):
        s_old = s_ref[0, h]
        o_ref[0, rows(ci), vcol(h)] = (ops[i]["o"] + _dot(qe_ref[i], s_old.astype(BF16))).astype(o_ref.dtype)
        s_ref[0, h] = e_ref[i] * s_old + u_ref[i]


def _gla_prompt(q, k, v, g):
    b, l, _ = q.shape
    blk = lambda w: pl.BlockSpec((1, GLA_BLOCK, w), lambda bi, i: (bi, i, 0))
    n = (GLA_BLOCK // GLA_CHUNK) * GLA_HEADS
    return pl.pallas_call(
        _gla_prompt_kernel,
        grid=(b, l // GLA_BLOCK),
        in_specs=[blk(GLA_KDIM), blk(GLA_KDIM), blk(GLA_VDIM), blk(GLA_KDIM)],
        out_specs=[blk(GLA_VDIM),
                   pl.BlockSpec((1, GLA_HEADS, GLA_DK, GLA_DV), lambda bi, i: (bi, 0, 0, 0))],
        out_shape=[jax.ShapeDtypeStruct((b, l, GLA_VDIM), BF16),
                   jax.ShapeDtypeStruct((b, GLA_HEADS, GLA_DK, GLA_DV), F32)],
        scratch_shapes=[pltpu.VMEM((n, GLA_DK, GLA_DV), F32), pltpu.VMEM((n, GLA_DK, GLA_DV), F32),
                        pltpu.VMEM((n, GLA_CHUNK, GLA_DK), BF16)],
        compiler_params=_params(2),
        name="gla_prompt",
    )(q, k, v, g)


def _gla_sample_kernel(q_ref, k_ref, v_ref, g_ref, s0_ref, *refs):
    o_ref, s_ref = refs[-2:]
    ng, t = q_ref.shape[:2]
    tril = _tri(t, lower=True)
    kcol = lambda h: slice(h * GLA_DK, (h + 1) * GLA_DK)
    vcol = lambda h: slice(h * GLA_DV, (h + 1) * GLA_DV)
    ids = [(s, h) for s in range(ng) for h in range(GLA_HEADS)]

    ops = []
    for s in range(ng):
        g = g_ref[s]
        rows = [g[0:1]]
        for i in range(1, t):
            rows.append(rows[-1] + g[i:i + 1])
        b = jnp.concatenate(rows, axis=0)
        b_last = rows[-1]
        q = q_ref[s].astype(F32)
        k = k_ref[s].astype(F32)
        ops.append(dict(qe=(q * jnp.exp(b)).astype(BF16), ke=(k * jnp.exp(-b)).astype(BF16),
                        kd=(k * jnp.exp(b_last - b)).astype(BF16), e=jnp.exp(b_last), v=v_ref[s]))
    a, upd, ecol = [], [], []
    for s, h in ids:
        d = ops[s]
        a.append(jnp.where(tril, _dot_nt(d["qe"][:, kcol(h)], d["ke"][:, kcol(h)]), 0.0).astype(BF16))
    for s, h in ids:
        d = ops[s]
        upd.append(_dot_tn(d["kd"][:, kcol(h)], d["v"][:, vcol(h)]))
        e_col = jnp.broadcast_to(d["e"][:, kcol(h)], (GLA_DK, GLA_DK)).T
        ecol.append(jnp.concatenate([e_col, e_col], axis=1))
    for i, (s, h) in enumerate(ids):
        d = ops[s]
        s_old = s0_ref[0, s, h]
        o_ref[s, :, vcol(h)] = (_dot(d["qe"][:, kcol(h)], s_old.astype(BF16))
                                + _dot(a[i], d["v"][:, vcol(h)]))
        s_ref[0, s, h] = ecol[i] * s_old + upd[i]


def _gla_sample(q, k, v, g, state, j, new_state=None):
    n, t, _ = q.shape
    ng = GLA_SAMPLE_GROUP
    blk = lambda w: pl.BlockSpec((ng, t, w), lambda i: (i, 0, 0))
    state_blk = pl.BlockSpec((1, ng, GLA_HEADS, GLA_DK, GLA_DV), lambda i: (j, i, 0, 0, 0))
    in_specs = [blk(GLA_KDIM), blk(GLA_KDIM), blk(GLA_VDIM), blk(GLA_KDIM), state_blk]
    args = [q, k, v, g, state]
    aliases = {}
    if new_state is not None:
        in_specs.append(pl.BlockSpec(memory_space=pl.ANY))
        args.append(new_state)
        aliases = {len(args) - 1: 1}
    return pl.pallas_call(
        _gla_sample_kernel,
        grid=(n // ng,),
        in_specs=in_specs,
        out_specs=[blk(GLA_VDIM), state_blk],
        out_shape=[jax.ShapeDtypeStruct((n, t, GLA_VDIM), F32),
                   jax.ShapeDtypeStruct(state.shape, F32)],
        input_output_aliases=aliases,
        compiler_params=_params(1),
        name="gla_sample",
    )(*args)


def _store_head_major(ref, y):
    rows = y.shape[0]
    for h in range(FOX_HEADS):
        ref[0, pl.ds(h, rows, stride=FOX_HEADS), :] = y[:, h * FOX_HD:(h + 1) * FOX_HD]


def _fox_proj_prompt_kernel(x_ref, gpre_ref, wq_ref, wk_ref, wv_ref, wz_ref, wf_ref, wft_ref, bf_ref,
                            bft_ref, gq_ref, gk_ref, *refs):
    kf_ref, vf_ref, qb_ref, kb_ref, vb_ref, z_ref, lf_ref, lft_ref = refs[-8:]
    hb = (_rms_rows(x_ref[...]) * gpre_ref[...]).astype(BF16)
    q = _group_rmsnorm(_dot(hb, wq_ref[...]), gq_ref[...], FOX_HD)
    qb_ref[...] = (q * (FOX_HD ** -0.5 * LOG2E)).astype(BF16)
    k = _group_rmsnorm(_dot(hb, wk_ref[...]), gk_ref[...], FOX_HD)
    kb_ref[...] = k.astype(BF16)
    _store_head_major(kf_ref, k)
    v = _dot(hb, wv_ref[...])
    vb_ref[...] = v.astype(BF16)
    _store_head_major(vf_ref, v)
    z_ref[...] = _dot(hb, wz_ref[...]).astype(BF16)
    lf_ref[...] = _log_sigmoid(_dot(hb, wf_ref[...]) + bf_ref[...])
    lft_ref[...] = _log_sigmoid(_dot_nt(wft_ref[...], hb) + bft_ref[...])


def _fox_proj_prompt(x, weights, j, n_layers, stacked=None):
    n = x.shape[0]
    tm = min(ROW_TILE, n)
    row = lambda w: pl.BlockSpec((tm, w), lambda i: (i, 0))
    fin = pl.BlockSpec((1, tm * FOX_HEADS, FOX_HD), lambda i: (j, i, 0))
    fin_shape = jax.ShapeDtypeStruct((n_layers, n * FOX_HEADS, FOX_HD), F32)
    wide_b = jax.ShapeDtypeStruct((n, FOX_WIDTH), BF16)
    in_specs = [row(D_MODEL)] + [_full(w.shape) for w in weights]
    args = [x, *weights]
    aliases = {}
    if stacked is not None:
        in_specs += [pl.BlockSpec(memory_space=pl.ANY)] * 2
        aliases = {len(args): 0, len(args) + 1: 1}
        args += list(stacked)
    return pl.pallas_call(
        _fox_proj_prompt_kernel,
        grid=(n // tm,),
        in_specs=in_specs,
        out_specs=[fin, fin, row(FOX_WIDTH), row(FOX_WIDTH), row(FOX_WIDTH), row(FOX_WIDTH),
                   row(FOX_HEADS), pl.BlockSpec((FOX_HEADS, tm), lambda i: (0, i))],
        out_shape=[fin_shape, fin_shape, wide_b, wide_b, wide_b, wide_b,
                   jax.ShapeDtypeStruct((n, FOX_HEADS), F32), jax.ShapeDtypeStruct((FOX_HEADS, n), F32)],
        input_output_aliases=aliases,
        compiler_params=_params(1),
        name="fox_proj_prompt",
    )(*args)


def _fox_proj_sample_kernel(x_ref, gpre_ref, wq_ref, wk_ref, wv_ref, wz_ref, wf_ref, wft_ref, bf_ref,
                            bft_ref, gq_ref, gk_ref, q_ref, k_ref, v_ref, z_ref, lf_ref):
    del wft_ref, bft_ref
    hb = (_rms_rows(x_ref[...]) * gpre_ref[...]).astype(BF16)
    _store_head_major(q_ref, _group_rmsnorm(_dot(hb, wq_ref[...]), gq_ref[...], FOX_HD))
    _store_head_major(k_ref, _group_rmsnorm(_dot(hb, wk_ref[...]), gk_ref[...], FOX_HD))
    _store_head_major(v_ref, _dot(hb, wv_ref[...]))
    z_ref[...] = _dot(hb, wz_ref[...]).astype(BF16)
    lf_ref[...] = _log_sigmoid(_dot(hb, wf_ref[...]) + bf_ref[...])


def _fox_proj_sample(x, weights):
    n = x.shape[0]
    tm = min(ROW_TILE, n)
    row = lambda w: pl.BlockSpec((tm, w), lambda i: (i, 0))
    fin = pl.BlockSpec((1, tm * FOX_HEADS, FOX_HD), lambda i: (0, i, 0))
    fin_shape = jax.ShapeDtypeStruct((1, n * FOX_HEADS, FOX_HD), F32)
    return pl.pallas_call(
        _fox_proj_sample_kernel,
        grid=(n // tm,),
        in_specs=[row(D_MODEL)] + [_full(w.shape) for w in weights],
        out_specs=[fin, fin, fin, row(FOX_WIDTH), row(FOX_HEADS)],
        out_shape=[fin_shape, fin_shape, fin_shape, jax.ShapeDtypeStruct((n, FOX_WIDTH), BF16),
                   jax.ShapeDtypeStruct((n, FOX_HEADS), F32)],
        compiler_params=_params(1),
        name="fox_proj_sample",
    )(x, *weights)


def _fox_cumsum_kernel(lf_ref, lft_ref, c_ref, crow_ref):
    l = lf_ref.shape[1]
    tril = _tri(LANES, lower=True).astype(F32)
    triu = _tri(LANES, lower=False).astype(F32)
    carry_row = jnp.zeros((1, FOX_HEADS), F32)
    carry_col = jnp.zeros((FOX_HEADS, 1), F32)
    for i in range(l // LANES):
        sl = slice(i * LANES, (i + 1) * LANES)
        c = _dot_exact(tril, lf_ref[0, sl, :]) + carry_row
        carry_row = c[LANES - 1:LANES, :]
        c_ref[0, sl, :] = c
        ct = _dot_exact(lft_ref[:, sl], triu) + carry_col
        carry_col = ct[:, LANES - 1:LANES]
        for h in range(FOX_HEADS):
            crow_ref[0, h, :, sl] = ct[h:h + 1, :]


def _fox_cumsum(lf, lft):
    b, l, _ = lf.shape
    return pl.pallas_call(
        _fox_cumsum_kernel,
        grid=(b,),
        in_specs=[pl.BlockSpec((1, l, FOX_HEADS), lambda i: (i, 0, 0)),
                  pl.BlockSpec((FOX_HEADS, l), lambda i: (0, i))],
        out_specs=[pl.BlockSpec((1, l, FOX_HEADS), lambda i: (i, 0, 0)),
                   pl.BlockSpec((1, FOX_HEADS, 1, l), lambda i: (i, 0, 0, 0))],
        out_shape=[jax.ShapeDtypeStruct((b, l, FOX_HEADS), F32),
                   jax.ShapeDtypeStruct((b, FOX_HEADS, 1, l), F32)],
        compiler_params=_params(1),
        name="fox_cumsum",
    )(lf, lft)


def _fox_flash_kernel(q_ref, k_ref, v_ref, c_ref, crow_ref, o_ref, ck_ref, m_ref, l_ref, acc_ref):
    h = pl.program_id(1)
    l = q_ref.shape[1]
    t = ATT_BLOCK
    nb = l // t
    lane = lax.broadcasted_iota(jnp.int32, (l, FOX_HEADS), 1)
    col = jnp.sum(jnp.where(lane == h, c_ref[0], 0.0), axis=-1, keepdims=True) * LOG2E
    ck_ref[...] = jnp.broadcast_to(col, (l, LANES))
    m_ref[...] = jnp.full_like(m_ref, -jnp.inf)
    l_ref[...] = jnp.zeros_like(l_ref)
    acc_ref[...] = jnp.zeros_like(acc_ref)
    causal = _tri(t, lower=False)

    blocks = [(ki, qb) for ki in range(nb) for qb in range(ki, nb)]
    keys = {}

    def key_side(ki):
        if ki not in keys:
            ks = slice(ki * t, (ki + 1) * t)
            keys[ki] = (k_ref[0, ks, :], v_ref[0, ks, :].astype(F32).T.astype(BF16),
                        jnp.tile(ck_ref[ks, :], (1, t // LANES)))
        return keys[ki]

    def scores(ki, qb):
        kb, _, ck = key_side(ki)
        st = _dot_nt(kb, q_ref[0, qb * t:(qb + 1) * t, :]) - ck
        return jnp.where(causal, st, -jnp.inf) if ki == qb else st

    def update(st, ki, qb):
        vt = key_side(ki)[1]
        cq = crow_ref[0, 0, qb:qb + 1, :] * LOG2E
        m_old = m_ref[qb]
        m_new = jnp.maximum(m_old, jnp.max(st, axis=0, keepdims=True) + cq)
        alpha = jnp.exp2(m_old - m_new)
        p = jnp.exp2(st - (m_new - cq))
        l_ref[qb] = alpha * l_ref[qb] + jnp.sum(p, axis=0, keepdims=True)
        acc_ref[qb] = alpha * acc_ref[qb] + _dot(vt, p.astype(BF16))
        m_ref[qb] = m_new

    st_next = scores(*blocks[0])
    for n, blk in enumerate(blocks):
        st = st_next
        if n + 1 < len(blocks):
            st_next = scores(*blocks[n + 1])
        update(st, *blk)

    for qb in range(nb):
        o_ref[0, qb * t:(qb + 1) * t, :] = (acc_ref[qb] / l_ref[qb]).T.astype(o_ref.dtype)


def _fox_flash(q, k, v, c, crow):
    b, l, _ = q.shape
    t = ATT_BLOCK
    nb = l // t
    head = pl.BlockSpec((1, l, FOX_HD), lambda bi, h: (bi, 0, h))
    return pl.pallas_call(
        _fox_flash_kernel,
        grid=(b, FOX_HEADS),
        in_specs=[head, head, head,
                  pl.BlockSpec((1, l, FOX_HEADS), lambda bi, h: (bi, 0, 0)),
                  pl.BlockSpec((1, 1, nb, t), lambda bi, h: (bi, h, 0, 0))],
        out_specs=head,
        out_shape=jax.ShapeDtypeStruct((b, l, FOX_WIDTH), BF16),
        scratch_shapes=[pltpu.VMEM((l, LANES), F32), pltpu.VMEM((nb, 1, t), F32),
                        pltpu.VMEM((nb, 1, t), F32), pltpu.VMEM((nb, FOX_HD, t), F32)],
        compiler_params=_params(2),
        name="fox_flash",
    )(q, k, v, c, crow.reshape(b, FOX_HEADS, nb, t))


def _fox_decode_kernel(n_pages, pt_ref, q_ref, kn_ref, vn_ref, lfn_ref, *refs):
    del pt_ref
    kp_refs = refs[:n_pages]
    vp_refs = refs[n_pages:2 * n_pages]
    lf_refs = refs[2 * n_pages:3 * n_pages]
    o_ref = refs[3 * n_pages]
    nq = q_ref.shape[1]
    n_rows = n_pages * ROWS_PER_PAGE

    li = lax.broadcasted_iota(jnp.int32, (LANES, LANES), 0)
    lj = lax.broadcasted_iota(jnp.int32, (LANES, LANES), 1)
    same_head = (li % FOX_HEADS) == (lj % FOX_HEADS)
    m_incl = (same_head & (li <= lj)).astype(F32)
    m_all = same_head.astype(F32)
    strict = (lax.broadcasted_iota(jnp.int32, (n_rows, n_rows), 0)
              > lax.broadcasted_iota(jnp.int32, (n_rows, n_rows), 1)).astype(F32)

    x = jnp.concatenate([lf_refs[p][0, 0] for p in range(n_pages)], axis=0)
    row_tot = _dot_exact(x, m_all)
    cp = _dot_exact(x, m_incl) + _dot_exact(strict, row_tot)
    past_bias = jnp.sum(row_tot, axis=0, keepdims=True) - cp
    cn_row = _dot_exact(lfn_ref[0], m_incl)
    qi = lax.broadcasted_iota(jnp.int32, (nq, LANES), 0)
    qj = lax.broadcasted_iota(jnp.int32, (nq, LANES), 1)
    cn_col = jnp.sum(jnp.where(qi == qj, jnp.broadcast_to(cn_row, (nq, LANES)), 0.0), axis=1, keepdims=True)

    head_ok = (qi % FOX_HEADS) == (qj % FOX_HEADS)
    qb = (q_ref[0] * (FOX_HD ** -0.5)).astype(BF16)

    chunks = []
    for p in range(n_pages):
        s = _dot_nt(qb, kp_refs[p][0, 0].astype(BF16))
        for r in range(ROWS_PER_PAGE):
            i = p * ROWS_PER_PAGE + r
            lg = s[:, r * LANES:(r + 1) * LANES] + past_bias[i:i + 1, :] + cn_col
            chunks.append(jnp.where(head_ok, lg, -jnp.inf))
    s_new = _dot_nt(qb, kn_ref[0].astype(BF16))
    ni = lax.broadcasted_iota(jnp.int32, (nq, nq), 0)
    nj = lax.broadcasted_iota(jnp.int32, (nq, nq), 1)
    causal = ((ni % FOX_HEADS) == (nj % FOX_HEADS)) & ((nj // FOX_HEADS) <= (ni // FOX_HEADS))
    l_new = jnp.where(causal, s_new + cn_col - cn_row[:, :nq], -jnp.inf)

    m = jnp.max(l_new, axis=-1, keepdims=True)
    mx = chunks[0]
    for c in chunks[1:]:
        mx = jnp.maximum(mx, c)
    m = jnp.maximum(m, jnp.max(mx, axis=-1, keepdims=True))
    p_new = jnp.exp(l_new - m)
    acc = _dot(p_new.astype(BF16), vn_ref[0].astype(BF16))
    den = jnp.zeros((nq, LANES), F32)
    for p in range(n_pages):
        ps = [jnp.exp(chunks[p * ROWS_PER_PAGE + r] - m) for r in range(ROWS_PER_PAGE)]
        for e in ps:
            den = den + e
        acc = acc + _dot(jnp.concatenate(ps, axis=1).astype(BF16), vp_refs[p][0, 0].astype(BF16))
    denom = jnp.sum(den, axis=-1, keepdims=True) + jnp.sum(p_new, axis=-1, keepdims=True)
    o_ref[0] = acc / denom


def _fox_decode(q, kn, vn, lfn, cache_k, cache_v, cache_lf, page_table, j):
    n, nq, _ = q.shape
    n_pages = page_table.shape[1]
    tok = pl.BlockSpec((1, nq, FOX_HD), lambda i, pt: (i, 0, 0))

    def page_spec(p, shape):
        return pl.BlockSpec((1, 1) + shape, lambda i, pt: (j, pt[i * n_pages + p], 0, 0))

    in_specs = ([tok, tok, tok, pl.BlockSpec((1, 1, LANES), lambda i, pt: (i, 0, 0))]
                + [page_spec(p, (PAGE_SIZE * FOX_HEADS, FOX_HD)) for p in range(n_pages)]
                + [page_spec(p, (PAGE_SIZE * FOX_HEADS, FOX_HD)) for p in range(n_pages)]
                + [page_spec(p, (ROWS_PER_PAGE, LANES)) for p in range(n_pages)])
    grid_spec = pltpu.PrefetchScalarGridSpec(
        num_scalar_prefetch=1, grid=(n,), in_specs=in_specs, out_specs=tok)
    return pl.pallas_call(
        functools.partial(_fox_decode_kernel, n_pages),
        grid_spec=grid_spec,
        out_shape=jax.ShapeDtypeStruct((n, nq, FOX_HD), F32),
        compiler_params=_params(1),
        name="fox_decode",
    )(page_table.reshape(-1), q, kn, vn, lfn,
      *([cache_k] * n_pages), *([cache_v] * n_pages), *([cache_lf] * n_pages))


def _out_kernel(group, o_ref, z_ref, x_ref, gh_ref, w_ref, gpost_ref, y_ref):
    gated = _group_rmsnorm(o_ref[...].astype(F32), gh_ref[...], group) * _silu(z_ref[...].astype(F32))
    y = _dot(gated.astype(BF16), w_ref[...])
    y_ref[...] = x_ref[...] + _rms_rows(y) * gpost_ref[...]


def _out_stage(o, z, x, gh, w, gpost, group):
    n = x.shape[0]
    tm = min(ROW_TILE, n)
    row = pl.BlockSpec((tm, D_MODEL), lambda i: (i, 0))
    return pl.pallas_call(
        functools.partial(_out_kernel, group),
        grid=(n // tm,),
        in_specs=[row, row, row, _full(gh.shape), _full(w.shape), _full(gpost.shape)],
        out_specs=row,
        out_shape=jax.ShapeDtypeStruct((n, D_MODEL), F32),
        compiler_params=_params(1),
        name="out_stage",
    )(o, z, x, gh, w, gpost)


def _gla_layer(xp, xs, state_gla, new_state, j, gpre, gpost, w_in, w_gate2, b_gate, g_out, w_out):
    bp, lp, _ = xp.shape
    bs, ls, _ = xs.shape
    kd, vd = GLA_KDIM, GLA_VDIM
    wb = w_in.astype(BF16)
    weights = (gpre.reshape(1, -1), wb[:, :kd], wb[:, kd:2 * kd], wb[:, 2 * kd:2 * kd + vd],
               wb[:, 2 * kd + vd:2 * kd + 2 * vd], wb[:, 2 * kd + 2 * vd:], w_gate2.astype(BF16),
               b_gate.reshape(1, -1))
    gh = jnp.tile(g_out, GLA_HEADS).reshape(1, -1)
    w_out_b = w_out.astype(BF16)
    gpost = gpost.reshape(1, -1)

    xp2 = xp.reshape(bp * lp, D_MODEL)
    q, k, v, z, g = _gla_proj(xp2, *weights)
    o, s_p = _gla_prompt(q.reshape(bp, lp, kd), k.reshape(bp, lp, kd), v.reshape(bp, lp, vd),
                         g.reshape(bp, lp, kd))
    yp = _out_stage(o.reshape(bp * lp, vd), z, xp2, gh, w_out_b, gpost, GLA_DV).reshape(bp, lp, D_MODEL)

    xs2 = xs.reshape(bs * ls, D_MODEL)
    q, k, v, z, g = _gla_proj(xs2, *weights)
    o, s_s = _gla_sample(q.reshape(bs, ls, kd), k.reshape(bs, ls, kd), v.reshape(bs, ls, vd),
                         g.reshape(bs, ls, kd), state_gla, j, new_state)
    ys = _out_stage(o.reshape(bs * ls, vd), z, xs2, gh, w_out_b, gpost, GLA_DV).reshape(bs, ls, D_MODEL)
    return yp, ys, s_p, s_s


def _fox_layer(xp, xs, cache_k, cache_v, cache_lf, page_table, kv_stacked, j, n_layers, gpre, gpost, w_in,
               b_f, g_q, g_k, g_o, w_out):
    bp, lp, _ = xp.shape
    bs, ls, _ = xs.shape
    w = FOX_WIDTH
    wb = w_in.astype(BF16)
    wf = wb[:, 4 * w:]
    weights = (gpre.reshape(1, -1), wb[:, :w], wb[:, w:2 * w], wb[:, 2 * w:3 * w], wb[:, 3 * w:4 * w],
               wf, wf.T, b_f.reshape(1, -1), b_f.reshape(-1, 1),
               jnp.tile(g_q, FOX_HEADS).reshape(1, -1), jnp.tile(g_k, FOX_HEADS).reshape(1, -1))
    gh = jnp.tile(g_o, FOX_HEADS).reshape(1, -1)
    w_out_b = w_out.astype(BF16)
    gpost = gpost.reshape(1, -1)

    xp2 = xp.reshape(bp * lp, D_MODEL)
    kf, vf, qb, kb, vb, z, lf, lft = _fox_proj_prompt(xp2, weights, j, n_layers, kv_stacked)
    c, crow = _fox_cumsum(lf.reshape(bp, lp, FOX_HEADS), lft)
    o = _fox_flash(qb.reshape(bp, lp, w), kb.reshape(bp, lp, w), vb.reshape(bp, lp, w), c, crow)
    yp = _out_stage(o.reshape(bp * lp, w), z, xp2, gh, w_out_b, gpost, FOX_HD).reshape(bp, lp, D_MODEL)

    xs2 = xs.reshape(bs * ls, D_MODEL)
    nq = ls * FOX_HEADS
    q, k, v, z, lfs = _fox_proj_sample(xs2, weights)
    lfn = jnp.pad(lfs.reshape(bs, 1, nq), ((0, 0), (0, 0), (0, LANES - nq)))
    o = _fox_decode(q.reshape(bs, nq, FOX_HD), k.reshape(bs, nq, FOX_HD), v.reshape(bs, nq, FOX_HD), lfn,
                    cache_k, cache_v, cache_lf, page_table, j)
    ys = _out_stage(o.reshape(bs * ls, w), z, xs2, gh, w_out_b, gpost, FOX_HD).reshape(bs, ls, D_MODEL)
    new_s = (k.reshape(bs, ls, FOX_HEADS, FOX_HD), v.reshape(bs, ls, FOX_HEADS, FOX_HD),
             lfs.reshape(bs, ls, FOX_HEADS))
    return yp, ys, (kf, vf), lf.reshape(bp, lp, FOX_HEADS), new_s


def kernel(x_prompt, x_sample, state_gla, cache_k, cache_v, cache_logf, page_table, norm_pre, norm_post,
           gla_w_in, gla_w_gate2, gla_b_gate, gla_norm_o, gla_w_out, fox_w_in, fox_b_f, fox_norm_q,
           fox_norm_k, fox_norm_o, fox_w_out):
    depth = norm_pre.shape[0]
    bp, lp, _ = x_prompt.shape
    n_fox, n_phys = cache_k.shape[:2]
    ck = cache_k.reshape(n_fox, n_phys, PAGE_SIZE * FOX_HEADS, FOX_HD)
    cv = cache_v.reshape(n_fox, n_phys, PAGE_SIZE * FOX_HEADS, FOX_HD)
    clf = cache_logf.reshape(n_fox, n_phys, ROWS_PER_PAGE, LANES)

    xp, xs = x_prompt, x_sample
    gla_p, fox_lf, fox_s = [], [], []
    gla_s = None
    kv_stacked = None
    for i in range(depth):
        j = i // 2
        if i % 2 == 0:
            xp, xs, s_p, gla_s = _gla_layer(xp, xs, state_gla, gla_s, j, norm_pre[i], norm_post[i],
                                            gla_w_in[j], gla_w_gate2[j], gla_b_gate[j], gla_norm_o[j],
                                            gla_w_out[j])
            gla_p.append(s_p)
        else:
            xp, xs, kv_stacked, lf, new_s = _fox_layer(xp, xs, ck, cv, clf, page_table, kv_stacked, j, n_fox,
                                                       norm_pre[i], norm_post[i], fox_w_in[j], fox_b_f[j],
                                                       fox_norm_q[j], fox_norm_k[j], fox_norm_o[j],
                                                       fox_w_out[j])
            fox_lf.append(lf)
            fox_s.append(new_s)
    kp, vp = (a.reshape(n_fox, bp, lp, FOX_HEADS, FOX_HD) for a in kv_stacked)
    ks, vs, fs = (jnp.stack(a) for a in zip(*fox_s))
    return (xp, xs, jnp.stack(gla_p), gla_s, kp, vp, jnp.stack(fox_lf), ks, vs, fs)
```

```python
import functools

import numpy as np
import jax
import jax.numpy as jnp
from jax import lax
from jax.experimental import pallas as pl
from jax.experimental.pallas import tpu as pltpu

F32 = jnp.float32
BF16 = jnp.bfloat16
EPS = 1e-6

D_MODEL = 1024
GLA_HEADS = 4
GLA_DK = 128
GLA_DV = 256
GLA_KDIM = GLA_HEADS * GLA_DK
GLA_VDIM = GLA_HEADS * GLA_DV
GLA_RANK = 16
GLA_TAU = 16.0
FOX_HEADS = 8
FOX_HD = 128
FOX_WIDTH = FOX_HEADS * FOX_HD
PAGE_SIZE = 128

V7X_VMEM_LIMIT_BYTES = 56 * 1024 * 1024
LANES = 128

LOG2E = 1.4426950408889634
ROWS_PER_PAGE = PAGE_SIZE * FOX_HEADS // LANES

ROW_TILE = 512
GLA_CHUNK = 128
GLA_HALF = GLA_CHUNK // 2
GLA_BLOCK = 512
GLA_SAMPLE_GROUP = 8
ATT_BLOCK = 512


def _params(n_axes):
    return pltpu.CompilerParams(
        dimension_semantics=("arbitrary",) * n_axes,
        vmem_limit_bytes=V7X_VMEM_LIMIT_BYTES,
    )


def _dot(a, b):
    return jnp.dot(a, b, preferred_element_type=F32)


def _dot_nt(a, b):
    return lax.dot_general(a, b, (((1,), (1,)), ((), ())), preferred_element_type=F32)


def _dot_tn(a, b):
    return lax.dot_general(a, b, (((0,), (0,)), ((), ())), preferred_element_type=F32)


def _dot_exact(a, b):
    return jnp.dot(a, b, precision=lax.Precision.HIGHEST, preferred_element_type=F32)


def _log_sigmoid(x):
    return jnp.minimum(x, 0.0) - jnp.log1p(jnp.exp(-jnp.abs(x)))


def _silu(x):
    return x * (1.0 / (1.0 + jnp.exp(-x)))


def _rms_rows(x):
    return x * lax.rsqrt(jnp.mean(x * x, axis=-1, keepdims=True) + EPS)


def _group_rmsnorm(y, gain_row, group):
    outs = []
    for g0 in range(0, y.shape[-1], group):
        yg = y[:, g0:g0 + group]
        outs.append(_rms_rows(yg) * gain_row[:, g0:g0 + group])
    return jnp.concatenate(outs, axis=-1)


def _out_value(group, o_ref, z_ref, x_ref, gh_ref, w_ref, gpost_ref, rows=slice(None)):
    gated = (_group_rmsnorm(o_ref[rows, :].astype(F32), gh_ref[...], group)
             * _silu(z_ref[rows, :].astype(F32)))
    y = _dot(gated.astype(BF16), w_ref[...])
    return x_ref[rows, :] + _rms_rows(y) * gpost_ref[...]


N_OUT_STAGE_INPUTS = 6
FUSED_ROW_SPLIT = 2


def _tile_chunks(pre_group, refs):
    if pre_group is None:
        return [(slice(None), refs[0][...])], refs[1:]
    step = refs[0].shape[0] // FUSED_ROW_SPLIT
    chunks = [slice(i * step, (i + 1) * step) for i in range(FUSED_ROW_SPLIT)]
    return ([(rows, _out_value(pre_group, *refs[:N_OUT_STAGE_INPUTS], rows=rows)) for rows in chunks],
            refs[N_OUT_STAGE_INPUTS:])


def _tri(n, lower):
    r = lax.broadcasted_iota(jnp.int32, (n, n), 0)
    c = lax.broadcasted_iota(jnp.int32, (n, n), 1)
    return (r >= c) if lower else (r <= c)


def _full(shape):
    nd = len(shape)
    return pl.BlockSpec(shape, lambda *_: (0,) * nd)


N_GLA_PROJ_WEIGHTS = 8


def _gla_proj_kernel(pre_group, *refs):
    chunks, refs = _tile_chunks(pre_group, refs)
    gpre_ref, wq_ref, wk_ref, wv_ref, wz_ref, wlr_ref, wg2_ref, bg_ref = refs[:N_GLA_PROJ_WEIGHTS]
    outs = refs[N_GLA_PROJ_WEIGHTS:]
    if pre_group is not None:
        for rows, x in chunks:
            outs[0][rows, :] = x
        outs = outs[1:]
    q_ref, k_ref, v_ref, z_ref, g_ref = outs
    for rows, x in chunks:
        hb = (_rms_rows(x) * gpre_ref[...]).astype(BF16)
        q_ref[rows, :] = (_dot(hb, wq_ref[...]) * (GLA_DK ** -0.5)).astype(BF16)
        k_ref[rows, :] = _dot(hb, wk_ref[...]).astype(BF16)
        v_ref[rows, :] = _dot(hb, wv_ref[...]).astype(BF16)
        z_ref[rows, :] = _dot(hb, wz_ref[...]).astype(BF16)
        lr = _dot(hb, wlr_ref[...])
        xg = _dot(lr.astype(BF16), wg2_ref[...]) + bg_ref[...]
        g_ref[rows, :] = _log_sigmoid(xg) * (1.0 / GLA_TAU)


def _row_inputs(x, pre, tm):
    row = pl.BlockSpec((tm, D_MODEL), lambda i: (i, 0))
    if pre is None:
        return None, [x], [row], [], []
    o, z, x_prev, gh, w, gpost, group = pre
    return (group, [o, z, x_prev, gh, w, gpost],
            [row, row, row, _full(gh.shape), _full(w.shape), _full(gpost.shape)],
            [row], [jax.ShapeDtypeStruct(x_prev.shape, F32)])


def _gla_proj(x, weights, pre=None):
    n = (x if pre is None else pre[2]).shape[0]
    tm = min(ROW_TILE, n)
    row = lambda w: pl.BlockSpec((tm, w), lambda i: (i, 0))
    group, lead, lead_specs, y_spec, y_shape = _row_inputs(x, pre, tm)
    return pl.pallas_call(
        functools.partial(_gla_proj_kernel, group),
        grid=(n // tm,),
        in_specs=lead_specs + [_full(w.shape) for w in weights],
        out_specs=y_spec + [row(GLA_KDIM), row(GLA_KDIM), row(GLA_VDIM), row(GLA_VDIM), row(GLA_KDIM)],
        out_shape=y_shape + [jax.ShapeDtypeStruct((n, GLA_KDIM), BF16), jax.ShapeDtypeStruct((n, GLA_KDIM), BF16),
                             jax.ShapeDtypeStruct((n, GLA_VDIM), BF16), jax.ShapeDtypeStruct((n, GLA_VDIM), BF16),
                             jax.ShapeDtypeStruct((n, GLA_KDIM), F32)],
        compiler_params=_params(1),
        name="gla_proj",
    )(*lead, *weights)


def _split3(x):
    x1 = x.astype(BF16)
    r1 = x - x1.astype(F32)
    x2 = r1.astype(BF16)
    x3 = (r1 - x2.astype(F32)).astype(BF16)
    return x1, x2, x3


def _prefix_sum_rows(tril_b, x):
    x1, x2, x3 = _split3(x)
    return _dot(tril_b, x1) + (_dot(tril_b, x2) + _dot(tril_b, x3))


def _gla_prompt_kernel(q_ref, k_ref, v_ref, g_ref, o_ref, s_ref, u_ref, e_ref, qe_ref):
    @pl.when(pl.program_id(1) == 0)
    def _():
        s_ref[...] = jnp.zeros_like(s_ref)

    c, hf = GLA_CHUNK, GLA_HALF
    nc = GLA_BLOCK // c
    tril_b = _tri(c, lower=True).astype(BF16)
    tril_h = _tri(hf, lower=True)
    ids = [(ci, h) for ci in range(nc) for h in range(GLA_HEADS)]
    rows = lambda ci: slice(ci * c, (ci + 1) * c)
    kcol = lambda h: slice(h * GLA_DK, (h + 1) * GLA_DK)
    vcol = lambda h: slice(h * GLA_DV, (h + 1) * GLA_DV)

    bs = [_prefix_sum_rows(tril_b, g_ref[0, rows(ci), :]) for ci in range(nc)]

    ops = []
    for ci, h in ids:
        b = bs[ci][:, kcol(h)]
        q = q_ref[0, rows(ci), kcol(h)].astype(F32)
        k = k_ref[0, rows(ci), kcol(h)].astype(F32)
        b0, b1 = b[:hf], b[hf:]
        r0 = b[hf // 2 - 1:hf // 2]
        rb = b[hf - 1:hf]
        r1 = b[hf + hf // 2 - 1:hf + hf // 2]
        b_last = b[c - 1:c]
        ops.append(dict(
            qe=(q * jnp.exp(b)).astype(BF16),
            kd=(k * jnp.exp(b_last - b)).astype(BF16),
            e=jnp.exp(b_last),
            q0=(q[:hf] * jnp.exp(b0 - r0)).astype(BF16), k0=(k[:hf] * jnp.exp(r0 - b0)).astype(BF16),
            q1=(q[hf:] * jnp.exp(b1 - r1)).astype(BF16), k1=(k[hf:] * jnp.exp(r1 - b1)).astype(BF16),
            q1b=(q[hf:] * jnp.exp(b1 - rb)).astype(BF16), k0b=(k[:hf] * jnp.exp(rb - b0)).astype(BF16)))

    for d in ops:
        a00 = jnp.where(tril_h, _dot_nt(d["q0"], d["k0"]), 0.0)
        a11 = jnp.where(tril_h, _dot_nt(d["q1"], d["k1"]), 0.0)
        a10 = _dot_nt(d["q1b"], d["k0b"])
        top = jnp.concatenate([a00, jnp.zeros_like(a00)], axis=1)
        bot = jnp.concatenate([a10, a11], axis=1)
        d["a"] = jnp.concatenate([top, bot], axis=0).astype(BF16)
    for i, (ci, h) in enumerate(ids):
        d = ops[i]
        v = v_ref[0, rows(ci), vcol(h)]
        d["o"] = _dot(d["a"], v)
        u_ref[i] = _dot_tn(d["kd"], v)
        e_col = jnp.broadcast_to(d["e"], (GLA_DK, GLA_DK)).T
        e_ref[i] = jnp.concatenate([e_col, e_col], axis=1)
        qe_ref[i] = d["qe"]

    for i, (ci, h) in enumerate(ids):
        s_old = s_ref[0, h]
        o_ref[0, rows(ci), vcol(h)] = (ops[i]["o"] + _dot(qe_ref[i], s_old.astype(BF16))).astype(o_ref.dtype)
        s_ref[0, h] = e_ref[i] * s_old + u_ref[i]


def _gla_prompt(q, k, v, g):
    b, l, _ = q.shape
    blk = lambda w: pl.BlockSpec((1, GLA_BLOCK, w), lambda bi, i: (bi, i, 0))
    n = (GLA_BLOCK // GLA_CHUNK) * GLA_HEADS
    return pl.pallas_call(
        _gla_prompt_kernel,
        grid=(b, l // GLA_BLOCK),
        in_specs=[blk(GLA_KDIM), blk(GLA_KDIM), blk(GLA_VDIM), blk(GLA_KDIM)],
        out_specs=[blk(GLA_VDIM),
                   pl.BlockSpec((1, GLA_HEADS, GLA_DK, GLA_DV), lambda bi, i: (bi, 0, 0, 0))],
        out_shape=[jax.ShapeDtypeStruct((b, l, GLA_VDIM), BF16),
                   jax.ShapeDtypeStruct((b, GLA_HEADS, GLA_DK, GLA_DV), F32)],
        scratch_shapes=[pltpu.VMEM((n, GLA_DK, GLA_DV), F32), pltpu.VMEM((n, GLA_DK, GLA_DV), F32),
                        pltpu.VMEM((n, GLA_CHUNK, GLA_DK), BF16)],
        compiler_params=_params(2),
        name="gla_prompt",
    )(q, k, v, g)


def _gla_sample_kernel(q_ref, k_ref, v_ref, g_ref, s0_ref, *refs):
    o_ref, s_ref = refs[-2:]
    ng, t = q_ref.shape[:2]
    tril = _tri(t, lower=True)
    kcol = lambda h: slice(h * GLA_DK, (h + 1) * GLA_DK)
    vcol = lambda h: slice(h * GLA_DV, (h + 1) * GLA_DV)
    ids = [(s, h) for s in range(ng) for h in range(GLA_HEADS)]

    ops = []
    for s in range(ng):
        g = g_ref[s]
        rows = [g[0:1]]
        for i in range(1, t):
            rows.append(rows[-1] + g[i:i + 1])
        b = jnp.concatenate(rows, axis=0)
        b_last = rows[-1]
        q = q_ref[s].astype(F32)
        k = k_ref[s].astype(F32)
        ops.append(dict(qe=(q * jnp.exp(b)).astype(BF16), ke=(k * jnp.exp(-b)).astype(BF16),
                        kd=(k * jnp.exp(b_last - b)).astype(BF16), e=jnp.exp(b_last), v=v_ref[s]))
    a, upd, ecol = [], [], []
    for s, h in ids:
        d = ops[s]
        a.append(jnp.where(tril, _dot_nt(d["qe"][:, kcol(h)], d["ke"][:, kcol(h)]), 0.0).astype(BF16))
    for s, h in ids:
        d = ops[s]
        upd.append(_dot_tn(d["kd"][:, kcol(h)], d["v"][:, vcol(h)]))
        e_col = jnp.broadcast_to(d["e"][:, kcol(h)], (GLA_DK, GLA_DK)).T
        ecol.append(jnp.concatenate([e_col, e_col], axis=1))
    for i, (s, h) in enumerate(ids):
        d = ops[s]
        s_old = s0_ref[0, s, h]
        o_ref[s, :, vcol(h)] = (_dot(d["qe"][:, kcol(h)], s_old.astype(BF16))
                                + _dot(a[i], d["v"][:, vcol(h)]))
        s_ref[0, s, h] = ecol[i] * s_old + upd[i]


def _gla_sample(q, k, v, g, state, j, new_state=None):
    n, t, _ = q.shape
    ng = GLA_SAMPLE_GROUP
    blk = lambda w: pl.BlockSpec((ng, t, w), lambda i: (i, 0, 0))
    state_blk = pl.BlockSpec((1, ng, GLA_HEADS, GLA_DK, GLA_DV), lambda i: (j, i, 0, 0, 0))
    in_specs = [blk(GLA_KDIM), blk(GLA_KDIM), blk(GLA_VDIM), blk(GLA_KDIM), state_blk]
    args = [q, k, v, g, state]
    aliases = {}
    if new_state is not None:
        in_specs.append(pl.BlockSpec(memory_space=pl.ANY))
        args.append(new_state)
        aliases = {len(args) - 1: 1}
    return pl.pallas_call(
        _gla_sample_kernel,
        grid=(n // ng,),
        in_specs=in_specs,
        out_specs=[blk(GLA_VDIM), state_blk],
        out_shape=[jax.ShapeDtypeStruct((n, t, GLA_VDIM), F32),
                   jax.ShapeDtypeStruct(state.shape, F32)],
        input_output_aliases=aliases,
        compiler_params=_params(1),
        name="gla_sample",
    )(*args)


def _store_head_major(ref, y, first_row=0):
    rows = y.shape[0]
    for h in range(FOX_HEADS):
        ref[0, pl.ds(first_row * FOX_HEADS + h, rows, stride=FOX_HEADS), :] = y[:, h * FOX_HD:(h + 1) * FOX_HD]


N_FOX_PROJ_WEIGHTS = 11
N_FOX_PROJ_OUTPUTS = 8


def _fox_proj_prompt_kernel(pre_group, *refs):
    chunks, refs = _tile_chunks(pre_group, refs)
    (gpre_ref, wq_ref, wk_ref, wv_ref, wz_ref, wf_ref, wft_ref, bf_ref, bft_ref, gq_ref,
     gk_ref) = refs[:N_FOX_PROJ_WEIGHTS]
    if pre_group is not None:
        for rows, x in chunks:
            refs[-N_FOX_PROJ_OUTPUTS - 1][rows, :] = x
    kf_ref, vf_ref, qb_ref, kb_ref, vb_ref, z_ref, lf_ref, lft_ref = refs[-N_FOX_PROJ_OUTPUTS:]
    for rows, x in chunks:
        first = rows.start or 0
        hb = (_rms_rows(x) * gpre_ref[...]).astype(BF16)
        q = _group_rmsnorm(_dot(hb, wq_ref[...]), gq_ref[...], FOX_HD)
        qb_ref[rows, :] = (q * (FOX_HD ** -0.5 * LOG2E)).astype(BF16)
        k = _group_rmsnorm(_dot(hb, wk_ref[...]), gk_ref[...], FOX_HD)
        kb_ref[rows, :] = k.astype(BF16)
        _store_head_major(kf_ref, k, first)
        v = _dot(hb, wv_ref[...])
        vb_ref[rows, :] = v.astype(BF16)
        _store_head_major(vf_ref, v, first)
        z_ref[rows, :] = _dot(hb, wz_ref[...]).astype(BF16)
        lf_ref[rows, :] = _log_sigmoid(_dot(hb, wf_ref[...]) + bf_ref[...])
        lft_ref[:, rows] = _log_sigmoid(_dot_nt(wft_ref[...], hb) + bft_ref[...])


def _fox_proj_prompt(x, weights, j, n_layers, stacked=None, pre=None):
    n = (x if pre is None else pre[2]).shape[0]
    tm = min(ROW_TILE, n)
    row = lambda w: pl.BlockSpec((tm, w), lambda i: (i, 0))
    fin = pl.BlockSpec((1, tm * FOX_HEADS, FOX_HD), lambda i: (j, i, 0))
    fin_shape = jax.ShapeDtypeStruct((n_layers, n * FOX_HEADS, FOX_HD), F32)
    wide_b = jax.ShapeDtypeStruct((n, FOX_WIDTH), BF16)
    group, lead, lead_specs, y_spec, y_shape = _row_inputs(x, pre, tm)
    in_specs = lead_specs + [_full(w.shape) for w in weights]
    args = [*lead, *weights]
    aliases = {}
    if stacked is not None:
        in_specs += [pl.BlockSpec(memory_space=pl.ANY)] * 2
        aliases = {len(args): len(y_spec), len(args) + 1: len(y_spec) + 1}
        args += list(stacked)
    return pl.pallas_call(
        functools.partial(_fox_proj_prompt_kernel, group),
        grid=(n // tm,),
        in_specs=in_specs,
        out_specs=y_spec + [fin, fin, row(FOX_WIDTH), row(FOX_WIDTH), row(FOX_WIDTH), row(FOX_WIDTH),
                            row(FOX_HEADS), pl.BlockSpec((FOX_HEADS, tm), lambda i: (0, i))],
        out_shape=y_shape + [fin_shape, fin_shape, wide_b, wide_b, wide_b, wide_b,
                             jax.ShapeDtypeStruct((n, FOX_HEADS), F32), jax.ShapeDtypeStruct((FOX_HEADS, n), F32)],
        input_output_aliases=aliases,
        compiler_params=_params(1),
        name="fox_proj_prompt",
    )(*args)


def _fox_proj_sample_kernel(x_ref, gpre_ref, wq_ref, wk_ref, wv_ref, wz_ref, wf_ref, wft_ref, bf_ref,
                            bft_ref, gq_ref, gk_ref, q_ref, k_ref, v_ref, z_ref, lf_ref):
    del wft_ref, bft_ref
    hb = (_rms_rows(x_ref[...]) * gpre_ref[...]).astype(BF16)
    _store_head_major(q_ref, _group_rmsnorm(_dot(hb, wq_ref[...]), gq_ref[...], FOX_HD))
    _store_head_major(k_ref, _group_rmsnorm(_dot(hb, wk_ref[...]), gk_ref[...], FOX_HD))
    _store_head_major(v_ref, _dot(hb, wv_ref[...]))
    z_ref[...] = _dot(hb, wz_ref[...]).astype(BF16)
    lf_ref[...] = _log_sigmoid(_dot(hb, wf_ref[...]) + bf_ref[...])


def _fox_proj_sample(x, weights):
    n = x.shape[0]
    tm = min(ROW_TILE, n)
    row = lambda w: pl.BlockSpec((tm, w), lambda i: (i, 0))
    fin = pl.BlockSpec((1, tm * FOX_HEADS, FOX_HD), lambda i: (0, i, 0))
    fin_shape = jax.ShapeDtypeStruct((1, n * FOX_HEADS, FOX_HD), F32)
    return pl.pallas_call(
        _fox_proj_sample_kernel,
        grid=(n // tm,),
        in_specs=[row(D_MODEL)] + [_full(w.shape) for w in weights],
        out_specs=[fin, fin, fin, row(FOX_WIDTH), row(FOX_HEADS)],
        out_shape=[fin_shape, fin_shape, fin_shape, jax.ShapeDtypeStruct((n, FOX_WIDTH), BF16),
                   jax.ShapeDtypeStruct((n, FOX_HEADS), F32)],
        compiler_params=_params(1),
        name="fox_proj_sample",
    )(x, *weights)


def _fox_cumsum_kernel(lf_ref, lft_ref, c_ref, crow_ref):
    l = lf_ref.shape[1]
    tril = _tri(LANES, lower=True).astype(F32)
    triu = _tri(LANES, lower=False).astype(F32)
    carry_row = jnp.zeros((1, FOX_HEADS), F32)
    carry_col = jnp.zeros((FOX_HEADS, 1), F32)
    for i in range(l // LANES):
        sl = slice(i * LANES, (i + 1) * LANES)
        c = _dot_exact(tril, lf_ref[0, sl, :]) + carry_row
        carry_row = c[LANES - 1:LANES, :]
        c_ref[0, sl, :] = c
        ct = _dot_exact(lft_ref[:, sl], triu) + carry_col
        carry_col = ct[:, LANES - 1:LANES]
        for h in range(FOX_HEADS):
            crow_ref[0, h, :, sl] = ct[h:h + 1, :]


def _fox_cumsum(lf, lft):
    b, l, _ = lf.shape
    return pl.pallas_call(
        _fox_cumsum_kernel,
        grid=(b,),
        in_specs=[pl.BlockSpec((1, l, FOX_HEADS), lambda i: (i, 0, 0)),
                  pl.BlockSpec((FOX_HEADS, l), lambda i: (0, i))],
        out_specs=[pl.BlockSpec((1, l, FOX_HEADS), lambda i: (i, 0, 0)),
                   pl.BlockSpec((1, FOX_HEADS, 1, l), lambda i: (i, 0, 0, 0))],
        out_shape=[jax.ShapeDtypeStruct((b, l, FOX_HEADS), F32),
                   jax.ShapeDtypeStruct((b, FOX_HEADS, 1, l), F32)],
        compiler_params=_params(1),
        name="fox_cumsum",
    )(lf, lft)


def _fox_flash_kernel(q_ref, k_ref, v_ref, c_ref, crow_ref, o_ref, ck_ref, m_ref, l_ref, acc_ref):
    h = pl.program_id(1)
    l = q_ref.shape[1]
    t = ATT_BLOCK
    nb = l // t
    lane = lax.broadcasted_iota(jnp.int32, (l, FOX_HEADS), 1)
    col = jnp.sum(jnp.where(lane == h, c_ref[0], 0.0), axis=-1, keepdims=True) * LOG2E
    ck_ref[...] = jnp.broadcast_to(col, (l, LANES))
    m_ref[...] = jnp.full_like(m_ref, -jnp.inf)
    l_ref[...] = jnp.zeros_like(l_ref)
    acc_ref[...] = jnp.zeros_like(acc_ref)
    causal = _tri(t, lower=False)

    blocks = [(ki, qb) for ki in range(nb) for qb in range(ki, nb)]
    keys = {}

    def key_side(ki):
        if ki not in keys:
            ks = slice(ki * t, (ki + 1) * t)
            keys[ki] = (k_ref[0, ks, :], v_ref[0, ks, :].astype(F32).T.astype(BF16),
                        jnp.tile(ck_ref[ks, :], (1, t // LANES)))
        return keys[ki]

    def scores(ki, qb):
        kb, _, ck = key_side(ki)
        st = _dot_nt(kb, q_ref[0, qb * t:(qb + 1) * t, :]) - ck
        return jnp.where(causal, st, -jnp.inf) if ki == qb else st

    def update(st, ki, qb):
        vt = key_side(ki)[1]
        cq = crow_ref[0, 0, qb:qb + 1, :] * LOG2E
        m_old = m_ref[qb]
        m_new = jnp.maximum(m_old, jnp.max(st, axis=0, keepdims=True) + cq)
        alpha = jnp.exp2(m_old - m_new)
        p = jnp.exp2(st - (m_new - cq))
        l_ref[qb] = alpha * l_ref[qb] + jnp.sum(p, axis=0, keepdims=True)
        acc_ref[qb] = alpha * acc_ref[qb] + _dot(vt, p.astype(BF16))
        m_ref[qb] = m_new

    st_next = scores(*blocks[0])
    for n, blk in enumerate(blocks):
        st = st_next
        if n + 1 < len(blocks):
            st_next = scores(*blocks[n + 1])
        update(st, *blk)

    for qb in range(nb):
        o_ref[0, qb * t:(qb + 1) * t, :] = (acc_ref[qb] / l_ref[qb]).T.astype(o_ref.dtype)


def _fox_flash(q, k, v, c, crow):
    b, l, _ = q.shape
    t = ATT_BLOCK
    nb = l // t
    head = pl.BlockSpec((1, l, FOX_HD), lambda bi, h: (bi, 0, h))
    return pl.pallas_call(
        _fox_flash_kernel,
        grid=(b, FOX_HEADS),
        in_specs=[head, head, head,
                  pl.BlockSpec((1, l, FOX_HEADS), lambda bi, h: (bi, 0, 0)),
                  pl.BlockSpec((1, 1, nb, t), lambda bi, h: (bi, h, 0, 0))],
        out_specs=head,
        out_shape=jax.ShapeDtypeStruct((b, l, FOX_WIDTH), BF16),
        scratch_shapes=[pltpu.VMEM((l, LANES), F32), pltpu.VMEM((nb, 1, t), F32),
                        pltpu.VMEM((nb, 1, t), F32), pltpu.VMEM((nb, FOX_HD, t), F32)],
        compiler_params=_params(2),
        name="fox_flash",
    )(q, k, v, c, crow.reshape(b, FOX_HEADS, nb, t))


def _fox_decode_kernel(n_pages, pt_ref, q_ref, kn_ref, vn_ref, lfn_ref, *refs):
    del pt_ref
    kp_refs = refs[:n_pages]
    vp_refs = refs[n_pages:2 * n_pages]
    lf_refs = refs[2 * n_pages:3 * n_pages]
    o_ref = refs[3 * n_pages]
    nq = q_ref.shape[1]
    n_rows = n_pages * ROWS_PER_PAGE

    li = lax.broadcasted_iota(jnp.int32, (LANES, LANES), 0)
    lj = lax.broadcasted_iota(jnp.int32, (LANES, LANES), 1)
    same_head = (li % FOX_HEADS) == (lj % FOX_HEADS)
    m_incl = (same_head & (li <= lj)).astype(F32)
    m_all = same_head.astype(F32)
    strict = (lax.broadcasted_iota(jnp.int32, (n_rows, n_rows), 0)
              > lax.broadcasted_iota(jnp.int32, (n_rows, n_rows), 1)).astype(F32)

    x = jnp.concatenate([lf_refs[p][0, 0] for p in range(n_pages)], axis=0)
    row_tot = _dot_exact(x, m_all)
    cp = _dot_exact(x, m_incl) + _dot_exact(strict, row_tot)
    past_bias = jnp.sum(row_tot, axis=0, keepdims=True) - cp
    cn_row = _dot_exact(lfn_ref[0], m_incl)
    qi = lax.broadcasted_iota(jnp.int32, (nq, LANES), 0)
    qj = lax.broadcasted_iota(jnp.int32, (nq, LANES), 1)
    cn_col = jnp.sum(jnp.where(qi == qj, jnp.broadcast_to(cn_row, (nq, LANES)), 0.0), axis=1, keepdims=True)

    head_ok = (qi % FOX_HEADS) == (qj % FOX_HEADS)
    qb = (q_ref[0] * (FOX_HD ** -0.5)).astype(BF16)

    chunks = []
    for p in range(n_pages):
        s = _dot_nt(qb, kp_refs[p][0, 0].astype(BF16))
        for r in range(ROWS_PER_PAGE):
            i = p * ROWS_PER_PAGE + r
            lg = s[:, r * LANES:(r + 1) * LANES] + past_bias[i:i + 1, :] + cn_col
            chunks.append(jnp.where(head_ok, lg, -jnp.inf))
    s_new = _dot_nt(qb, kn_ref[0].astype(BF16))
    ni = lax.broadcasted_iota(jnp.int32, (nq, nq), 0)
    nj = lax.broadcasted_iota(jnp.int32, (nq, nq), 1)
    causal = ((ni % FOX_HEADS) == (nj % FOX_HEADS)) & ((nj // FOX_HEADS) <= (ni // FOX_HEADS))
    l_new = jnp.where(causal, s_new + cn_col - cn_row[:, :nq], -jnp.inf)

    m = jnp.max(l_new, axis=-1, keepdims=True)
    mx = chunks[0]
    for c in chunks[1:]:
        mx = jnp.maximum(mx, c)
    m = jnp.maximum(m, jnp.max(mx, axis=-1, keepdims=True))
    p_new = jnp.exp(l_new - m)
    acc = _dot(p_new.astype(BF16), vn_ref[0].astype(BF16))
    den = jnp.zeros((nq, LANES), F32)
    for p in range(n_pages):
        ps = [jnp.exp(chunks[p * ROWS_PER_PAGE + r] - m) for r in range(ROWS_PER_PAGE)]
        for e in ps:
            den = den + e
        acc = acc + _dot(jnp.concatenate(ps, axis=1).astype(BF16), vp_refs[p][0, 0].astype(BF16))
    denom = jnp.sum(den, axis=-1, keepdims=True) + jnp.sum(p_new, axis=-1, keepdims=True)
    o_ref[0] = acc / denom


def _fox_decode(q, kn, vn, lfn, cache_k, cache_v, cache_lf, page_table, j):
    n, nq, _ = q.shape
    n_pages = page_table.shape[1]
    tok = pl.BlockSpec((1, nq, FOX_HD), lambda i, pt: (i, 0, 0))

    def page_spec(p, shape):
        return pl.BlockSpec((1, 1) + shape, lambda i, pt: (j, pt[i * n_pages + p], 0, 0))

    in_specs = ([tok, tok, tok, pl.BlockSpec((1, 1, LANES), lambda i, pt: (i, 0, 0))]
                + [page_spec(p, (PAGE_SIZE * FOX_HEADS, FOX_HD)) for p in range(n_pages)]
                + [page_spec(p, (PAGE_SIZE * FOX_HEADS, FOX_HD)) for p in range(n_pages)]
                + [page_spec(p, (ROWS_PER_PAGE, LANES)) for p in range(n_pages)])
    grid_spec = pltpu.PrefetchScalarGridSpec(
        num_scalar_prefetch=1, grid=(n,), in_specs=in_specs, out_specs=tok)
    return pl.pallas_call(
        functools.partial(_fox_decode_kernel, n_pages),
        grid_spec=grid_spec,
        out_shape=jax.ShapeDtypeStruct((n, nq, FOX_HD), F32),
        compiler_params=_params(1),
        name="fox_decode",
    )(page_table.reshape(-1), q, kn, vn, lfn,
      *([cache_k] * n_pages), *([cache_v] * n_pages), *([cache_lf] * n_pages))


def _out_kernel(group, o_ref, z_ref, x_ref, gh_ref, w_ref, gpost_ref, y_ref):
    y_ref[...] = _out_value(group, o_ref, z_ref, x_ref, gh_ref, w_ref, gpost_ref)


def _out_stage(o, z, x, gh, w, gpost, group):
    n = x.shape[0]
    tm = min(ROW_TILE, n)
    row = pl.BlockSpec((tm, D_MODEL), lambda i: (i, 0))
    return pl.pallas_call(
        functools.partial(_out_kernel, group),
        grid=(n // tm,),
        in_specs=[row, row, row, _full(gh.shape), _full(w.shape), _full(gpost.shape)],
        out_specs=row,
        out_shape=jax.ShapeDtypeStruct((n, D_MODEL), F32),
        compiler_params=_params(1),
        name="out_stage",
    )(o, z, x, gh, w, gpost)


def _gla_layer(xp, pre, dims, xs, state_gla, new_state, j, gpre, gpost, w_in, w_gate2, b_gate, g_out, w_out):
    bp, lp = dims
    bs, ls, _ = xs.shape
    kd, vd = GLA_KDIM, GLA_VDIM
    wb = w_in.astype(BF16)
    weights = (gpre.reshape(1, -1), wb[:, :kd], wb[:, kd:2 * kd], wb[:, 2 * kd:2 * kd + vd],
               wb[:, 2 * kd + vd:2 * kd + 2 * vd], wb[:, 2 * kd + 2 * vd:], w_gate2.astype(BF16),
               b_gate.reshape(1, -1))
    gh = jnp.tile(g_out, GLA_HEADS).reshape(1, -1)
    w_out_b = w_out.astype(BF16)
    gpost = gpost.reshape(1, -1)

    if pre is None:
        q, k, v, z, g = _gla_proj(xp, weights)
    else:
        xp, q, k, v, z, g = _gla_proj(None, weights, pre)
    o, s_p = _gla_prompt(q.reshape(bp, lp, kd), k.reshape(bp, lp, kd), v.reshape(bp, lp, vd),
                         g.reshape(bp, lp, kd))
    pre_next = (o.reshape(bp * lp, vd), z, xp, gh, w_out_b, gpost, GLA_DV)

    xs2 = xs.reshape(bs * ls, D_MODEL)
    q, k, v, z, g = _gla_proj(xs2, weights)
    o, s_s = _gla_sample(q.reshape(bs, ls, kd), k.reshape(bs, ls, kd), v.reshape(bs, ls, vd),
                         g.reshape(bs, ls, kd), state_gla, j, new_state)
    ys = _out_stage(o.reshape(bs * ls, vd), z, xs2, gh, w_out_b, gpost, GLA_DV).reshape(bs, ls, D_MODEL)
    return pre_next, ys, s_p, s_s


def _fox_layer(xp, pre, dims, xs, cache_k, cache_v, cache_lf, page_table, kv_stacked, j, n_layers, gpre, gpost,
               w_in, b_f, g_q, g_k, g_o, w_out):
    bp, lp = dims
    bs, ls, _ = xs.shape
    w = FOX_WIDTH
    wb = w_in.astype(BF16)
    wf = wb[:, 4 * w:]
    weights = (gpre.reshape(1, -1), wb[:, :w], wb[:, w:2 * w], wb[:, 2 * w:3 * w], wb[:, 3 * w:4 * w],
               wf, wf.T, b_f.reshape(1, -1), b_f.reshape(-1, 1),
               jnp.tile(g_q, FOX_HEADS).reshape(1, -1), jnp.tile(g_k, FOX_HEADS).reshape(1, -1))
    gh = jnp.tile(g_o, FOX_HEADS).reshape(1, -1)
    w_out_b = w_out.astype(BF16)
    gpost = gpost.reshape(1, -1)

    if pre is None:
        kf, vf, qb, kb, vb, z, lf, lft = _fox_proj_prompt(xp, weights, j, n_layers, kv_stacked)
    else:
        xp, kf, vf, qb, kb, vb, z, lf, lft = _fox_proj_prompt(None, weights, j, n_layers, kv_stacked, pre)
    c, crow = _fox_cumsum(lf.reshape(bp, lp, FOX_HEADS), lft)
    o = _fox_flash(qb.reshape(bp, lp, w), kb.reshape(bp, lp, w), vb.reshape(bp, lp, w), c, crow)
    pre_next = (o.reshape(bp * lp, w), z, xp, gh, w_out_b, gpost, FOX_HD)

    xs2 = xs.reshape(bs * ls, D_MODEL)
    nq = ls * FOX_HEADS
    q, k, v, z, lfs = _fox_proj_sample(xs2, weights)
    lfn = jnp.pad(lfs.reshape(bs, 1, nq), ((0, 0), (0, 0), (0, LANES - nq)))
    o = _fox_decode(q.reshape(bs, nq, FOX_HD), k.reshape(bs, nq, FOX_HD), v.reshape(bs, nq, FOX_HD), lfn,
                    cache_k, cache_v, cache_lf, page_table, j)
    ys = _out_stage(o.reshape(bs * ls, w), z, xs2, gh, w_out_b, gpost, FOX_HD).reshape(bs, ls, D_MODEL)
    new_s = (k.reshape(bs, ls, FOX_HEADS, FOX_HD), v.reshape(bs, ls, FOX_HEADS, FOX_HD),
             lfs.reshape(bs, ls, FOX_HEADS))
    return pre_next, ys, (kf, vf), lf.reshape(bp, lp, FOX_HEADS), new_s


def kernel(x_prompt, x_sample, state_gla, cache_k, cache_v, cache_logf, page_table, norm_pre, norm_post,
           gla_w_in, gla_w_gate2, gla_b_gate, gla_norm_o, gla_w_out, fox_w_in, fox_b_f, fox_norm_q,
           fox_norm_k, fox_norm_o, fox_w_out):
    depth = norm_pre.shape[0]
    bp, lp, _ = x_prompt.shape
    n_fox, n_phys = cache_k.shape[:2]
    ck = cache_k.reshape(n_fox, n_phys, PAGE_SIZE * FOX_HEADS, FOX_HD)
    cv = cache_v.reshape(n_fox, n_phys, PAGE_SIZE * FOX_HEADS, FOX_HD)
    clf = cache_logf.reshape(n_fox, n_phys, ROWS_PER_PAGE, LANES)

    xp, xs = x_prompt.reshape(bp * lp, D_MODEL), x_sample
    pre = None
    gla_p, fox_lf, fox_s = [], [], []
    gla_s = None
    kv_stacked = None
    for i in range(depth):
        j = i // 2
        if i % 2 == 0:
            pre, xs, s_p, gla_s = _gla_layer(xp, pre, (bp, lp), xs, state_gla, gla_s, j, norm_pre[i],
                                             norm_post[i], gla_w_in[j], gla_w_gate2[j], gla_b_gate[j],
                                             gla_norm_o[j], gla_w_out[j])
            gla_p.append(s_p)
        else:
            pre, xs, kv_stacked, lf, new_s = _fox_layer(xp, pre, (bp, lp), xs, ck, cv, clf, page_table,
                                                        kv_stacked, j, n_fox, norm_pre[i], norm_post[i],
                                                        fox_w_in[j], fox_b_f[j], fox_norm_q[j], fox_norm_k[j],
                                                        fox_norm_o[j], fox_w_out[j])
            fox_lf.append(lf)
            fox_s.append(new_s)
        xp = None
    yp = _out_stage(*pre).reshape(bp, lp, D_MODEL)
    kp, vp = (a.reshape(n_fox, bp, lp, FOX_HEADS, FOX_HD) for a in kv_stacked)
    ks, vs, fs = (jnp.stack(a) for a in zip(*fox_s))
    return (yp, xs, jnp.stack(gla_p), gla_s, kp, vp, jnp.stack(fox_lf), ks, vs, fs)
```

```python
import functools

import numpy as np
import jax
import jax.numpy as jnp
from jax import lax
from jax.experimental import pallas as pl
from jax.experimental.pallas import tpu as pltpu

F32 = jnp.float32
BF16 = jnp.bfloat16
EPS = 1e-6

D_MODEL = 1024
GLA_HEADS = 4
GLA_DK = 128
GLA_DV = 256
GLA_KDIM = GLA_HEADS * GLA_DK
GLA_VDIM = GLA_HEADS * GLA_DV
GLA_RANK = 16
GLA_TAU = 16.0
FOX_HEADS = 8
FOX_HD = 128
FOX_WIDTH = FOX_HEADS * FOX_HD
PAGE_SIZE = 128

V7X_VMEM_LIMIT_BYTES = 56 * 1024 * 1024
LANES = 128

LOG2E = 1.4426950408889634
ROWS_PER_PAGE = PAGE_SIZE * FOX_HEADS // LANES

ROW_TILE = 512
GLA_CHUNK = 128
GLA_HALF = GLA_CHUNK // 2
GLA_BLOCK = 512
GLA_SAMPLE_GROUP = 8
ATT_BLOCK = 512


def _params(n_axes):
    return pltpu.CompilerParams(
        dimension_semantics=("arbitrary",) * n_axes,
        vmem_limit_bytes=V7X_VMEM_LIMIT_BYTES,
    )


def _dot(a, b):
    return jnp.dot(a, b, preferred_element_type=F32)


def _dot_nt(a, b):
    return lax.dot_general(a, b, (((1,), (1,)), ((), ())), preferred_element_type=F32)


def _dot_tn(a, b):
    return lax.dot_general(a, b, (((0,), (0,)), ((), ())), preferred_element_type=F32)


def _dot_exact(a, b):
    return jnp.dot(a, b, precision=lax.Precision.HIGHEST, preferred_element_type=F32)


def _log_sigmoid(x):
    return jnp.minimum(x, 0.0) - jnp.log1p(jnp.exp(-jnp.abs(x)))


def _silu(x):
    return x * (1.0 / (1.0 + jnp.exp(-x)))


def _rms_rows(x):
    return x * lax.rsqrt(jnp.mean(x * x, axis=-1, keepdims=True) + EPS)


def _group_rmsnorm(y, gain_row, group):
    outs = []
    for g0 in range(0, y.shape[-1], group):
        yg = y[:, g0:g0 + group]
        outs.append(_rms_rows(yg) * gain_row[:, g0:g0 + group])
    return jnp.concatenate(outs, axis=-1)


def _out_value(group, o_ref, z_ref, x_ref, gh_ref, w_ref, gpost_ref, rows=slice(None)):
    gated = (_group_rmsnorm(o_ref[rows, :].astype(F32), gh_ref[...], group)
             * _silu(z_ref[rows, :].astype(F32)))
    y = _dot(gated.astype(BF16), w_ref[...])
    return x_ref[rows, :] + _rms_rows(y) * gpost_ref[...]


N_OUT_STAGE_INPUTS = 6
FUSED_ROW_SPLIT = 2


def _tile_chunks(pre_group, refs):
    if pre_group is None:
        return [(slice(None), refs[0][...])], refs[1:]
    step = refs[0].shape[0] // FUSED_ROW_SPLIT
    chunks = [slice(i * step, (i + 1) * step) for i in range(FUSED_ROW_SPLIT)]
    return ([(rows, _out_value(pre_group, *refs[:N_OUT_STAGE_INPUTS], rows=rows)) for rows in chunks],
            refs[N_OUT_STAGE_INPUTS:])


def _tri(n, lower):
    r = lax.broadcasted_iota(jnp.int32, (n, n), 0)
    c = lax.broadcasted_iota(jnp.int32, (n, n), 1)
    return (r >= c) if lower else (r <= c)


def _full(shape):
    nd = len(shape)
    return pl.BlockSpec(shape, lambda *_: (0,) * nd)


class _Cols:
    def __init__(self, array, width, index):
        self.array, self.width, self.index = array, width, index


def _weight_spec(w):
    if isinstance(w, _Cols):
        return pl.BlockSpec((w.array.shape[0], w.width), lambda *_: (0, w.index))
    return _full(w.shape)


def _weight_args(weights):
    return [w.array if isinstance(w, _Cols) else w for w in weights]


N_GLA_PROJ_WEIGHTS = 8


def _gla_proj_kernel(pre_group, *refs):
    chunks, refs = _tile_chunks(pre_group, refs)
    gpre_ref, wq_ref, wk_ref, wv_ref, wz_ref, wlr_ref, wg2_ref, bg_ref = refs[:N_GLA_PROJ_WEIGHTS]
    outs = refs[N_GLA_PROJ_WEIGHTS:]
    if pre_group is not None:
        for rows, x in chunks:
            outs[0][rows, :] = x
        outs = outs[1:]
    q_ref, k_ref, v_ref, z_ref, g_ref = outs
    for rows, x in chunks:
        hb = (_rms_rows(x) * gpre_ref[...]).astype(BF16)
        q_ref[rows, :] = (_dot(hb, wq_ref[...]) * (GLA_DK ** -0.5)).astype(BF16)
        k_ref[rows, :] = _dot(hb, wk_ref[...]).astype(BF16)
        v_ref[rows, :] = _dot(hb, wv_ref[...]).astype(BF16)
        z_ref[rows, :] = _dot(hb, wz_ref[...]).astype(BF16)
        lr = _dot(hb, wlr_ref[...])
        xg = _dot(lr.astype(BF16), wg2_ref[...]) + bg_ref[...]
        g_ref[rows, :] = _log_sigmoid(xg) * (1.0 / GLA_TAU)


def _row_inputs(x, pre, tm):
    row = pl.BlockSpec((tm, D_MODEL), lambda i: (i, 0))
    if pre is None:
        return None, [x], [row], [], []
    o, z, x_prev, gh, w, gpost, group = pre
    return (group, [o, z, x_prev, gh, w, gpost],
            [row, row, row, _full(gh.shape), _full(w.shape), _full(gpost.shape)],
            [row], [jax.ShapeDtypeStruct(x_prev.shape, F32)])


def _gla_proj(x, weights, pre=None):
    n = (x if pre is None else pre[2]).shape[0]
    tm = min(ROW_TILE, n)
    row = lambda w: pl.BlockSpec((tm, w), lambda i: (i, 0))
    group, lead, lead_specs, y_spec, y_shape = _row_inputs(x, pre, tm)
    return pl.pallas_call(
        functools.partial(_gla_proj_kernel, group),
        grid=(n // tm,),
        in_specs=lead_specs + [_weight_spec(w) for w in weights],
        out_specs=y_spec + [row(GLA_KDIM), row(GLA_KDIM), row(GLA_VDIM), row(GLA_VDIM), row(GLA_KDIM)],
        out_shape=y_shape + [jax.ShapeDtypeStruct((n, GLA_KDIM), BF16), jax.ShapeDtypeStruct((n, GLA_KDIM), BF16),
                             jax.ShapeDtypeStruct((n, GLA_VDIM), BF16), jax.ShapeDtypeStruct((n, GLA_VDIM), BF16),
                             jax.ShapeDtypeStruct((n, GLA_KDIM), F32)],
        compiler_params=_params(1),
        name="gla_proj",
    )(*lead, *_weight_args(weights))


def _split3(x):
    x1 = x.astype(BF16)
    r1 = x - x1.astype(F32)
    x2 = r1.astype(BF16)
    x3 = (r1 - x2.astype(F32)).astype(BF16)
    return x1, x2, x3


def _prefix_sum_rows(tril_b, x):
    x1, x2, x3 = _split3(x)
    return _dot(tril_b, x1) + (_dot(tril_b, x2) + _dot(tril_b, x3))


def _gla_prompt_kernel(q_ref, k_ref, v_ref, g_ref, o_ref, s_ref, u_ref, e_ref, qe_ref):
    @pl.when(pl.program_id(1) == 0)
    def _():
        s_ref[...] = jnp.zeros_like(s_ref)

    c, hf = GLA_CHUNK, GLA_HALF
    nc = GLA_BLOCK // c
    tril_b = _tri(c, lower=True).astype(BF16)
    tril_h = _tri(hf, lower=True)
    ids = [(ci, h) for ci in range(nc) for h in range(GLA_HEADS)]
    rows = lambda ci: slice(ci * c, (ci + 1) * c)
    kcol = lambda h: slice(h * GLA_DK, (h + 1) * GLA_DK)
    vcol = lambda h: slice(h * GLA_DV, (h + 1) * GLA_DV)

    bs = [_prefix_sum_rows(tril_b, g_ref[0, rows(ci), :]) for ci in range(nc)]

    ops = []
    for ci, h in ids:
        b = bs[ci][:, kcol(h)]
        q = q_ref[0, rows(ci), kcol(h)].astype(F32)
        k = k_ref[0, rows(ci), kcol(h)].astype(F32)
        b0, b1 = b[:hf], b[hf:]
        r0 = b[hf // 2 - 1:hf // 2]
        rb = b[hf - 1:hf]
        r1 = b[hf + hf // 2 - 1:hf + hf // 2]
        b_last = b[c - 1:c]
        ops.append(dict(
            qe=(q * jnp.exp(b)).astype(BF16),
            kd=(k * jnp.exp(b_last - b)).astype(BF16),
            e=jnp.exp(b_last),
            q0=(q[:hf] * jnp.exp(b0 - r0)).astype(BF16), k0=(k[:hf] * jnp.exp(r0 - b0)).astype(BF16),
            q1=(q[hf:] * jnp.exp(b1 - r1)).astype(BF16), k1=(k[hf:] * jnp.exp(r1 - b1)).astype(BF16),
            q1b=(q[hf:] * jnp.exp(b1 - rb)).astype(BF16), k0b=(k[:hf] * jnp.exp(rb - b0)).astype(BF16)))

    for d in ops:
        a00 = jnp.where(tril_h, _dot_nt(d["q0"], d["k0"]), 0.0)
        a11 = jnp.where(tril_h, _dot_nt(d["q1"], d["k1"]), 0.0)
        a10 = _dot_nt(d["q1b"], d["k0b"])
        top = jnp.concatenate([a00, jnp.zeros_like(a00)], axis=1)
        bot = jnp.concatenate([a10, a11], axis=1)
        d["a"] = jnp.concatenate([top, bot], axis=0).astype(BF16)
    for i, (ci, h) in enumerate(ids):
        d = ops[i]
        v = v_ref[0, rows(ci), vcol(h)]
        d["o"] = _dot(d["a"], v)
        u_ref[i] = _dot_tn(d["kd"], v)
        e_col = jnp.broadcast_to(d["e"], (GLA_DK, GLA_DK)).T
        e_ref[i] = jnp.concatenate([e_col, e_col], axis=1)
        qe_ref[i] = d["qe"]

    for i, (ci, h) in enumerate(ids):
        s_old = s_ref[0, h]
        o_ref[0, rows(ci), vcol(h)] = (ops[i]["o"] + _dot(qe_ref[i], s_old.astype(BF16))).astype(o_ref.dtype)
        s_ref[0, h] = e_ref[i] * s_old + u_ref[i]


def _gla_prompt(q, k, v, g):
    b, l, _ = q.shape
    blk = lambda w: pl.BlockSpec((1, GLA_BLOCK, w), lambda bi, i: (bi, i, 0))
    n = (GLA_BLOCK // GLA_CHUNK) * GLA_HEADS
    return pl.pallas_call(
        _gla_prompt_kernel,
        grid=(b, l // GLA_BLOCK),
        in_specs=[blk(GLA_KDIM), blk(GLA_KDIM), blk(GLA_VDIM), blk(GLA_KDIM)],
        out_specs=[blk(GLA_VDIM),
                   pl.BlockSpec((1, GLA_HEADS, GLA_DK, GLA_DV), lambda bi, i: (bi, 0, 0, 0))],
        out_shape=[jax.ShapeDtypeStruct((b, l, GLA_VDIM), BF16),
                   jax.ShapeDtypeStruct((b, GLA_HEADS, GLA_DK, GLA_DV), F32)],
        scratch_shapes=[pltpu.VMEM((n, GLA_DK, GLA_DV), F32), pltpu.VMEM((n, GLA_DK, GLA_DV), F32),
                        pltpu.VMEM((n, GLA_CHUNK, GLA_DK), BF16)],
        compiler_params=_params(2),
        name="gla_prompt",
    )(q, k, v, g)


def _gla_sample_kernel(q_ref, k_ref, v_ref, g_ref, s0_ref, *refs):
    o_ref, s_ref = refs[-2:]
    ng, t = q_ref.shape[:2]
    tril = _tri(t, lower=True)
    kcol = lambda h: slice(h * GLA_DK, (h + 1) * GLA_DK)
    vcol = lambda h: slice(h * GLA_DV, (h + 1) * GLA_DV)
    ids = [(s, h) for s in range(ng) for h in range(GLA_HEADS)]

    ops = []
    for s in range(ng):
        g = g_ref[s]
        rows = [g[0:1]]
        for i in range(1, t):
            rows.append(rows[-1] + g[i:i + 1])
        b = jnp.concatenate(rows, axis=0)
        b_last = rows[-1]
        q = q_ref[s].astype(F32)
        k = k_ref[s].astype(F32)
        ops.append(dict(qe=(q * jnp.exp(b)).astype(BF16), ke=(k * jnp.exp(-b)).astype(BF16),
                        kd=(k * jnp.exp(b_last - b)).astype(BF16), e=jnp.exp(b_last), v=v_ref[s]))
    a, upd, ecol = [], [], []
    for s, h in ids:
        d = ops[s]
        a.append(jnp.where(tril, _dot_nt(d["qe"][:, kcol(h)], d["ke"][:, kcol(h)]), 0.0).astype(BF16))
    for s, h in ids:
        d = ops[s]
        upd.append(_dot_tn(d["kd"][:, kcol(h)], d["v"][:, vcol(h)]))
        e_col = jnp.broadcast_to(d["e"][:, kcol(h)], (GLA_DK, GLA_DK)).T
        ecol.append(jnp.concatenate([e_col, e_col], axis=1))
    for i, (s, h) in enumerate(ids):
        d = ops[s]
        s_old = s0_ref[0, s, h]
        o_ref[s, :, vcol(h)] = (_dot(d["qe"][:, kcol(h)], s_old.astype(BF16))
                                + _dot(a[i], d["v"][:, vcol(h)]))
        s_ref[0, s, h] = ecol[i] * s_old + upd[i]


def _gla_sample(q, k, v, g, state, j, new_state=None):
    n, t, _ = q.shape
    ng = GLA_SAMPLE_GROUP
    blk = lambda w: pl.BlockSpec((ng, t, w), lambda i: (i, 0, 0))
    state_blk = pl.BlockSpec((1, ng, GLA_HEADS, GLA_DK, GLA_DV), lambda i: (j, i, 0, 0, 0))
    in_specs = [blk(GLA_KDIM), blk(GLA_KDIM), blk(GLA_VDIM), blk(GLA_KDIM), state_blk]
    args = [q, k, v, g, state]
    aliases = {}
    if new_state is not None:
        in_specs.append(pl.BlockSpec(memory_space=pl.ANY))
        args.append(new_state)
        aliases = {len(args) - 1: 1}
    return pl.pallas_call(
        _gla_sample_kernel,
        grid=(n // ng,),
        in_specs=in_specs,
        out_specs=[blk(GLA_VDIM), state_blk],
        out_shape=[jax.ShapeDtypeStruct((n, t, GLA_VDIM), F32),
                   jax.ShapeDtypeStruct(state.shape, F32)],
        input_output_aliases=aliases,
        compiler_params=_params(1),
        name="gla_sample",
    )(*args)


def _store_head_major(ref, y, first_row=0):
    rows = y.shape[0]
    for h in range(FOX_HEADS):
        ref[0, pl.ds(first_row * FOX_HEADS + h, rows, stride=FOX_HEADS), :] = y[:, h * FOX_HD:(h + 1) * FOX_HD]


N_FOX_PROJ_WEIGHTS = 11
N_FOX_PROJ_OUTPUTS = 8


def _fox_proj_prompt_kernel(pre_group, *refs):
    chunks, refs = _tile_chunks(pre_group, refs)
    (gpre_ref, wq_ref, wk_ref, wv_ref, wz_ref, wf_ref, wft_ref, bf_ref, bft_ref, gq_ref,
     gk_ref) = refs[:N_FOX_PROJ_WEIGHTS]
    if pre_group is not None:
        for rows, x in chunks:
            refs[-N_FOX_PROJ_OUTPUTS - 1][rows, :] = x
    kf_ref, vf_ref, qb_ref, kb_ref, vb_ref, z_ref, lf_ref, lft_ref = refs[-N_FOX_PROJ_OUTPUTS:]
    for rows, x in chunks:
        first = rows.start or 0
        hb = (_rms_rows(x) * gpre_ref[...]).astype(BF16)
        q = _group_rmsnorm(_dot(hb, wq_ref[...]), gq_ref[...], FOX_HD)
        qb_ref[rows, :] = (q * (FOX_HD ** -0.5 * LOG2E)).astype(BF16)
        k = _group_rmsnorm(_dot(hb, wk_ref[...]), gk_ref[...], FOX_HD)
        kb_ref[rows, :] = k.astype(BF16)
        _store_head_major(kf_ref, k, first)
        v = _dot(hb, wv_ref[...])
        vb_ref[rows, :] = v.astype(BF16)
        _store_head_major(vf_ref, v, first)
        z_ref[rows, :] = _dot(hb, wz_ref[...]).astype(BF16)
        lf_ref[rows, :] = _log_sigmoid(_dot(hb, wf_ref[...]) + bf_ref[...])
        lft_ref[:, rows] = _log_sigmoid(_dot_nt(wft_ref[...], hb) + bft_ref[...])


def _fox_proj_prompt(x, weights, j, n_layers, stacked=None, pre=None):
    n = (x if pre is None else pre[2]).shape[0]
    tm = min(ROW_TILE, n)
    row = lambda w: pl.BlockSpec((tm, w), lambda i: (i, 0))
    fin = pl.BlockSpec((1, tm * FOX_HEADS, FOX_HD), lambda i: (j, i, 0))
    fin_shape = jax.ShapeDtypeStruct((n_layers, n * FOX_HEADS, FOX_HD), F32)
    wide_b = jax.ShapeDtypeStruct((n, FOX_WIDTH), BF16)
    group, lead, lead_specs, y_spec, y_shape = _row_inputs(x, pre, tm)
    in_specs = lead_specs + [_weight_spec(w) for w in weights]
    args = [*lead, *_weight_args(weights)]
    aliases = {}
    if stacked is not None:
        in_specs += [pl.BlockSpec(memory_space=pl.ANY)] * 2
        aliases = {len(args): len(y_spec), len(args) + 1: len(y_spec) + 1}
        args += list(stacked)
    return pl.pallas_call(
        functools.partial(_fox_proj_prompt_kernel, group),
        grid=(n // tm,),
        in_specs=in_specs,
        out_specs=y_spec + [fin, fin, row(FOX_WIDTH), row(FOX_WIDTH), row(FOX_WIDTH), row(FOX_WIDTH),
                            row(FOX_HEADS), pl.BlockSpec((FOX_HEADS, tm), lambda i: (0, i))],
        out_shape=y_shape + [fin_shape, fin_shape, wide_b, wide_b, wide_b, wide_b,
                             jax.ShapeDtypeStruct((n, FOX_HEADS), F32), jax.ShapeDtypeStruct((FOX_HEADS, n), F32)],
        input_output_aliases=aliases,
        compiler_params=_params(1),
        name="fox_proj_prompt",
    )(*args)


def _fox_proj_sample_kernel(x_ref, gpre_ref, wq_ref, wk_ref, wv_ref, wz_ref, wf_ref, wft_ref, bf_ref,
                            bft_ref, gq_ref, gk_ref, q_ref, k_ref, v_ref, z_ref, lf_ref):
    del wft_ref, bft_ref
    hb = (_rms_rows(x_ref[...]) * gpre_ref[...]).astype(BF16)
    _store_head_major(q_ref, _group_rmsnorm(_dot(hb, wq_ref[...]), gq_ref[...], FOX_HD))
    _store_head_major(k_ref, _group_rmsnorm(_dot(hb, wk_ref[...]), gk_ref[...], FOX_HD))
    _store_head_major(v_ref, _dot(hb, wv_ref[...]))
    z_ref[...] = _dot(hb, wz_ref[...]).astype(BF16)
    lf_ref[...] = _log_sigmoid(_dot(hb, wf_ref[...]) + bf_ref[...])


def _fox_proj_sample(x, weights):
    n = x.shape[0]
    tm = min(ROW_TILE, n)
    row = lambda w: pl.BlockSpec((tm, w), lambda i: (i, 0))
    fin = pl.BlockSpec((1, tm * FOX_HEADS, FOX_HD), lambda i: (0, i, 0))
    fin_shape = jax.ShapeDtypeStruct((1, n * FOX_HEADS, FOX_HD), F32)
    return pl.pallas_call(
        _fox_proj_sample_kernel,
        grid=(n // tm,),
        in_specs=[row(D_MODEL)] + [_weight_spec(w) for w in weights],
        out_specs=[fin, fin, fin, row(FOX_WIDTH), row(FOX_HEADS)],
        out_shape=[fin_shape, fin_shape, fin_shape, jax.ShapeDtypeStruct((n, FOX_WIDTH), BF16),
                   jax.ShapeDtypeStruct((n, FOX_HEADS), F32)],
        compiler_params=_params(1),
        name="fox_proj_sample",
    )(x, *_weight_args(weights))


def _fox_cumsum_kernel(lf_ref, lft_ref, c_ref, crow_ref):
    l = lf_ref.shape[1]
    tril = _tri(LANES, lower=True).astype(F32)
    triu = _tri(LANES, lower=False).astype(F32)
    carry_row = jnp.zeros((1, FOX_HEADS), F32)
    carry_col = jnp.zeros((FOX_HEADS, 1), F32)
    for i in range(l // LANES):
        sl = slice(i * LANES, (i + 1) * LANES)
        c = _dot_exact(tril, lf_ref[0, sl, :]) + carry_row
        carry_row = c[LANES - 1:LANES, :]
        c_ref[0, sl, :] = c
        ct = _dot_exact(lft_ref[:, sl], triu) + carry_col
        carry_col = ct[:, LANES - 1:LANES]
        for h in range(FOX_HEADS):
            crow_ref[0, h, :, sl] = ct[h:h + 1, :]


def _fox_cumsum(lf, lft):
    b, l, _ = lf.shape
    return pl.pallas_call(
        _fox_cumsum_kernel,
        grid=(b,),
        in_specs=[pl.BlockSpec((1, l, FOX_HEADS), lambda i: (i, 0, 0)),
                  pl.BlockSpec((FOX_HEADS, l), lambda i: (0, i))],
        out_specs=[pl.BlockSpec((1, l, FOX_HEADS), lambda i: (i, 0, 0)),
                   pl.BlockSpec((1, FOX_HEADS, 1, l), lambda i: (i, 0, 0, 0))],
        out_shape=[jax.ShapeDtypeStruct((b, l, FOX_HEADS), F32),
                   jax.ShapeDtypeStruct((b, FOX_HEADS, 1, l), F32)],
        compiler_params=_params(1),
        name="fox_cumsum",
    )(lf, lft)


def _fox_flash_kernel(q_ref, k_ref, v_ref, c_ref, crow_ref, o_ref, ck_ref, m_ref, l_ref, acc_ref):
    h = pl.program_id(1)
    l = q_ref.shape[1]
    t = ATT_BLOCK
    nb = l // t
    lane = lax.broadcasted_iota(jnp.int32, (l, FOX_HEADS), 1)
    col = jnp.sum(jnp.where(lane == h, c_ref[0], 0.0), axis=-1, keepdims=True) * LOG2E
    ck_ref[...] = jnp.broadcast_to(col, (l, LANES))
    m_ref[...] = jnp.full_like(m_ref, -jnp.inf)
    l_ref[...] = jnp.zeros_like(l_ref)
    acc_ref[...] = jnp.zeros_like(acc_ref)
    causal = _tri(t, lower=False)

    blocks = [(ki, qb) for ki in range(nb) for qb in range(ki, nb)]
    keys = {}

    def key_side(ki):
        if ki not in keys:
            ks = slice(ki * t, (ki + 1) * t)
            keys[ki] = (k_ref[0, ks, :], v_ref[0, ks, :].astype(F32).T.astype(BF16),
                        jnp.tile(ck_ref[ks, :], (1, t // LANES)))
        return keys[ki]

    def scores(ki, qb):
        kb, _, ck = key_side(ki)
        st = _dot_nt(kb, q_ref[0, qb * t:(qb + 1) * t, :]) - ck
        return jnp.where(causal, st, -jnp.inf) if ki == qb else st

    def update(st, ki, qb):
        vt = key_side(ki)[1]
        cq = crow_ref[0, 0, qb:qb + 1, :] * LOG2E
        m_old = m_ref[qb]
        m_new = jnp.maximum(m_old, jnp.max(st, axis=0, keepdims=True) + cq)
        alpha = jnp.exp2(m_old - m_new)
        p = jnp.exp2(st - (m_new - cq))
        l_ref[qb] = alpha * l_ref[qb] + jnp.sum(p, axis=0, keepdims=True)
        acc_ref[qb] = alpha * acc_ref[qb] + _dot(vt, p.astype(BF16))
        m_ref[qb] = m_new

    st_next = scores(*blocks[0])
    for n, blk in enumerate(blocks):
        st = st_next
        if n + 1 < len(blocks):
            st_next = scores(*blocks[n + 1])
        update(st, *blk)

    for qb in range(nb):
        o_ref[0, qb * t:(qb + 1) * t, :] = (acc_ref[qb] / l_ref[qb]).T.astype(o_ref.dtype)


def _fox_flash(q, k, v, c, crow):
    b, l, _ = q.shape
    t = ATT_BLOCK
    nb = l // t
    head = pl.BlockSpec((1, l, FOX_HD), lambda bi, h: (bi, 0, h))
    return pl.pallas_call(
        _fox_flash_kernel,
        grid=(b, FOX_HEADS),
        in_specs=[head, head, head,
                  pl.BlockSpec((1, l, FOX_HEADS), lambda bi, h: (bi, 0, 0)),
                  pl.BlockSpec((1, 1, nb, t), lambda bi, h: (bi, h, 0, 0))],
        out_specs=head,
        out_shape=jax.ShapeDtypeStruct((b, l, FOX_WIDTH), BF16),
        scratch_shapes=[pltpu.VMEM((l, LANES), F32), pltpu.VMEM((nb, 1, t), F32),
                        pltpu.VMEM((nb, 1, t), F32), pltpu.VMEM((nb, FOX_HD, t), F32)],
        compiler_params=_params(2),
        name="fox_flash",
    )(q, k, v, c, crow.reshape(b, FOX_HEADS, nb, t))


def _fox_decode_kernel(n_pages, pt_ref, q_ref, kn_ref, vn_ref, lfn_ref, *refs):
    del pt_ref
    kp_refs = refs[:n_pages]
    vp_refs = refs[n_pages:2 * n_pages]
    lf_refs = refs[2 * n_pages:3 * n_pages]
    o_ref = refs[3 * n_pages]
    nq = q_ref.shape[1]
    n_rows = n_pages * ROWS_PER_PAGE

    li = lax.broadcasted_iota(jnp.int32, (LANES, LANES), 0)
    lj = lax.broadcasted_iota(jnp.int32, (LANES, LANES), 1)
    same_head = (li % FOX_HEADS) == (lj % FOX_HEADS)
    m_incl = (same_head & (li <= lj)).astype(F32)
    m_all = same_head.astype(F32)
    strict = (lax.broadcasted_iota(jnp.int32, (n_rows, n_rows), 0)
              > lax.broadcasted_iota(jnp.int32, (n_rows, n_rows), 1)).astype(F32)

    x = jnp.concatenate([lf_refs[p][0, 0] for p in range(n_pages)], axis=0)
    row_tot = _dot_exact(x, m_all)
    cp = _dot_exact(x, m_incl) + _dot_exact(strict, row_tot)
    past_bias = jnp.sum(row_tot, axis=0, keepdims=True) - cp
    cn_row = _dot_exact(lfn_ref[0], m_incl)
    qi = lax.broadcasted_iota(jnp.int32, (nq, LANES), 0)
    qj = lax.broadcasted_iota(jnp.int32, (nq, LANES), 1)
    cn_col = jnp.sum(jnp.where(qi == qj, jnp.broadcast_to(cn_row, (nq, LANES)), 0.0), axis=1, keepdims=True)

    head_ok = (qi % FOX_HEADS) == (qj % FOX_HEADS)
    qb = (q_ref[0] * (FOX_HD ** -0.5)).astype(BF16)

    chunks = []
    for p in range(n_pages):
        s = _dot_nt(qb, kp_refs[p][0, 0].astype(BF16))
        for r in range(ROWS_PER_PAGE):
            i = p * ROWS_PER_PAGE + r
            lg = s[:, r * LANES:(r + 1) * LANES] + past_bias[i:i + 1, :] + cn_col
            chunks.append(jnp.where(head_ok, lg, -jnp.inf))
    s_new = _dot_nt(qb, kn_ref[0].astype(BF16))
    ni = lax.broadcasted_iota(jnp.int32, (nq, nq), 0)
    nj = lax.broadcasted_iota(jnp.int32, (nq, nq), 1)
    causal = ((ni % FOX_HEADS) == (nj % FOX_HEADS)) & ((nj // FOX_HEADS) <= (ni // FOX_HEADS))
    l_new = jnp.where(causal, s_new + cn_col - cn_row[:, :nq], -jnp.inf)

    m = jnp.max(l_new, axis=-1, keepdims=True)
    mx = chunks[0]
    for c in chunks[1:]:
        mx = jnp.maximum(mx, c)
    m = jnp.maximum(m, jnp.max(mx, axis=-1, keepdims=True))
    p_new = jnp.exp(l_new - m)
    acc = _dot(p_new.astype(BF16), vn_ref[0].astype(BF16))
    den = jnp.zeros((nq, LANES), F32)
    for p in range(n_pages):
        ps = [jnp.exp(chunks[p * ROWS_PER_PAGE + r] - m) for r in range(ROWS_PER_PAGE)]
        for e in ps:
            den = den + e
        acc = acc + _dot(jnp.concatenate(ps, axis=1).astype(BF16), vp_refs[p][0, 0].astype(BF16))
    denom = jnp.sum(den, axis=-1, keepdims=True) + jnp.sum(p_new, axis=-1, keepdims=True)
    o_ref[0] = acc / denom


def _fox_decode(q, kn, vn, lfn, cache_k, cache_v, cache_lf, page_table, j):
    n, nq, _ = q.shape
    n_pages = page_table.shape[1]
    tok = pl.BlockSpec((1, nq, FOX_HD), lambda i, pt: (i, 0, 0))

    def page_spec(p, shape):
        return pl.BlockSpec((1, 1) + shape, lambda i, pt: (j, pt[i * n_pages + p], 0, 0))

    in_specs = ([tok, tok, tok, pl.BlockSpec((1, 1, LANES), lambda i, pt: (i, 0, 0))]
                + [page_spec(p, (PAGE_SIZE * FOX_HEADS, FOX_HD)) for p in range(n_pages)]
                + [page_spec(p, (PAGE_SIZE * FOX_HEADS, FOX_HD)) for p in range(n_pages)]
                + [page_spec(p, (ROWS_PER_PAGE, LANES)) for p in range(n_pages)])
    grid_spec = pltpu.PrefetchScalarGridSpec(
        num_scalar_prefetch=1, grid=(n,), in_specs=in_specs, out_specs=tok)
    return pl.pallas_call(
        functools.partial(_fox_decode_kernel, n_pages),
        grid_spec=grid_spec,
        out_shape=jax.ShapeDtypeStruct((n, nq, FOX_HD), F32),
        compiler_params=_params(1),
        name="fox_decode",
    )(page_table.reshape(-1), q, kn, vn, lfn,
      *([cache_k] * n_pages), *([cache_v] * n_pages), *([cache_lf] * n_pages))


def _out_kernel(group, o_ref, z_ref, x_ref, gh_ref, w_ref, gpost_ref, y_ref):
    y_ref[...] = _out_value(group, o_ref, z_ref, x_ref, gh_ref, w_ref, gpost_ref)


def _out_stage(o, z, x, gh, w, gpost, group):
    n = x.shape[0]
    tm = min(ROW_TILE, n)
    row = pl.BlockSpec((tm, D_MODEL), lambda i: (i, 0))
    return pl.pallas_call(
        functools.partial(_out_kernel, group),
        grid=(n // tm,),
        in_specs=[row, row, row, _full(gh.shape), _full(w.shape), _full(gpost.shape)],
        out_specs=row,
        out_shape=jax.ShapeDtypeStruct((n, D_MODEL), F32),
        compiler_params=_params(1),
        name="out_stage",
    )(o, z, x, gh, w, gpost)


def _gla_layer(xp, pre, dims, xs, state_gla, new_state, j, gpre, gpost, w_in, w_gate2, b_gate, g_out, w_out):
    bp, lp = dims
    bs, ls, _ = xs.shape
    kd, vd = GLA_KDIM, GLA_VDIM
    wb = w_in.astype(BF16)
    weights = (gpre.reshape(1, -1), _Cols(wb, kd, 0), _Cols(wb, kd, 1), _Cols(wb, vd, 1), _Cols(wb, vd, 2),
               wb[:, 2 * kd + 2 * vd:], w_gate2.astype(BF16), b_gate.reshape(1, -1))
    gh = jnp.tile(g_out, GLA_HEADS).reshape(1, -1)
    w_out_b = w_out.astype(BF16)
    gpost = gpost.reshape(1, -1)

    if pre is None:
        q, k, v, z, g = _gla_proj(xp, weights)
    else:
        xp, q, k, v, z, g = _gla_proj(None, weights, pre)
    o, s_p = _gla_prompt(q.reshape(bp, lp, kd), k.reshape(bp, lp, kd), v.reshape(bp, lp, vd),
                         g.reshape(bp, lp, kd))
    pre_next = (o.reshape(bp * lp, vd), z, xp, gh, w_out_b, gpost, GLA_DV)

    xs2 = xs.reshape(bs * ls, D_MODEL)
    q, k, v, z, g = _gla_proj(xs2, weights)
    o, s_s = _gla_sample(q.reshape(bs, ls, kd), k.reshape(bs, ls, kd), v.reshape(bs, ls, vd),
                         g.reshape(bs, ls, kd), state_gla, j, new_state)
    ys = _out_stage(o.reshape(bs * ls, vd), z, xs2, gh, w_out_b, gpost, GLA_DV).reshape(bs, ls, D_MODEL)
    return pre_next, ys, s_p, s_s


def _fox_layer(xp, pre, dims, xs, cache_k, cache_v, cache_lf, page_table, kv_stacked, j, n_layers, gpre, gpost,
               w_in, b_f, g_q, g_k, g_o, w_out):
    bp, lp = dims
    bs, ls, _ = xs.shape
    w = FOX_WIDTH
    wb = w_in.astype(BF16)
    wf = wb[:, 4 * w:]
    weights = (gpre.reshape(1, -1), _Cols(wb, w, 0), _Cols(wb, w, 1), _Cols(wb, w, 2), _Cols(wb, w, 3),
               wf, wf.T, b_f.reshape(1, -1), b_f.reshape(-1, 1),
               jnp.tile(g_q, FOX_HEADS).reshape(1, -1), jnp.tile(g_k, FOX_HEADS).reshape(1, -1))
    gh = jnp.tile(g_o, FOX_HEADS).reshape(1, -1)
    w_out_b = w_out.astype(BF16)
    gpost = gpost.reshape(1, -1)

    if pre is None:
        kf, vf, qb, kb, vb, z, lf, lft = _fox_proj_prompt(xp, weights, j, n_layers, kv_stacked)
    else:
        xp, kf, vf, qb, kb, vb, z, lf, lft = _fox_proj_prompt(None, weights, j, n_layers, kv_stacked, pre)
    c, crow = _fox_cumsum(lf.reshape(bp, lp, FOX_HEADS), lft)
    o = _fox_flash(qb.reshape(bp, lp, w), kb.reshape(bp, lp, w), vb.reshape(bp, lp, w), c, crow)
    pre_next = (o.reshape(bp * lp, w), z, xp, gh, w_out_b, gpost, FOX_HD)

    xs2 = xs.reshape(bs * ls, D_MODEL)
    nq = ls * FOX_HEADS
    q, k, v, z, lfs = _fox_proj_sample(xs2, weights)
    lfn = jnp.pad(lfs.reshape(bs, 1, nq), ((0, 0), (0, 0), (0, LANES - nq)))
    o = _fox_decode(q.reshape(bs, nq, FOX_HD), k.reshape(bs, nq, FOX_HD), v.reshape(bs, nq, FOX_HD), lfn,
                    cache_k, cache_v, cache_lf, page_table, j)
    ys = _out_stage(o.reshape(bs * ls, w), z, xs2, gh, w_out_b, gpost, FOX_HD).reshape(bs, ls, D_MODEL)
    new_s = (k.reshape(bs, ls, FOX_HEADS, FOX_HD), v.reshape(bs, ls, FOX_HEADS, FOX_HD),
             lfs.reshape(bs, ls, FOX_HEADS))
    return pre_next, ys, (kf, vf), lf.reshape(bp, lp, FOX_HEADS), new_s


def kernel(x_prompt, x_sample, state_gla, cache_k, cache_v, cache_logf, page_table, norm_pre, norm_post,
           gla_w_in, gla_w_gate2, gla_b_gate, gla_norm_o, gla_w_out, fox_w_in, fox_b_f, fox_norm_q,
           fox_norm_k, fox_norm_o, fox_w_out):
    depth = norm_pre.shape[0]
    bp, lp, _ = x_prompt.shape
    n_fox, n_phys = cache_k.shape[:2]
    ck = cache_k.reshape(n_fox, n_phys, PAGE_SIZE * FOX_HEADS, FOX_HD)
    cv = cache_v.reshape(n_fox, n_phys, PAGE_SIZE * FOX_HEADS, FOX_HD)
    clf = cache_logf.reshape(n_fox, n_phys, ROWS_PER_PAGE, LANES)

    xp, xs = x_prompt.reshape(bp * lp, D_MODEL), x_sample
    pre = None
    gla_p, fox_lf, fox_s = [], [], []
    gla_s = None
    kv_stacked = None
    for i in range(depth):
        j = i // 2
        if i % 2 == 0:
            pre, xs, s_p, gla_s = _gla_layer(xp, pre, (bp, lp), xs, state_gla, gla_s, j, norm_pre[i],
                                             norm_post[i], gla_w_in[j], gla_w_gate2[j], gla_b_gate[j],
                                             gla_norm_o[j], gla_w_out[j])
            gla_p.append(s_p)
        else:
            pre, xs, kv_stacked, lf, new_s = _fox_layer(xp, pre, (bp, lp), xs, ck, cv, clf, page_table,
                                                        kv_stacked, j, n_fox, norm_pre[i], norm_post[i],
                                                        fox_w_in[j], fox_b_f[j], fox_norm_q[j], fox_norm_k[j],
                                                        fox_norm_o[j], fox_w_out[j])
            fox_lf.append(lf)
            fox_s.append(new_s)
        xp = None
    yp = _out_stage(*pre).reshape(bp, lp, D_MODEL)
    kp, vp = (a.reshape(n_fox, bp, lp, FOX_HEADS, FOX_HD) for a in kv_stacked)
    ks, vs, fs = (jnp.stack(a) for a in zip(*fox_s))
    return (yp, xs, jnp.stack(gla_p), gla_s, kp, vp, jnp.stack(fox_lf), ks, vs, fs)
```
